```python
import math
import jax, jax.numpy as jnp
from jax import lax
import numpy as np

D_MODEL = 1024
BATCH = 2
SEQ = 16384
DEPTH = 2

CTX_LEN = 256
GRID_W = 64
N_MIXERS = 2
DA_HEADS = 8
DA_HEAD_DIM = 64
DA_V_DIM = 2 * DA_HEAD_DIM
DA_QK_WIDTH = DA_HEADS * 2 * DA_HEAD_DIM
DA_V_WIDTH = DA_HEADS * DA_V_DIM
Q_BLOCK = 128
ROPE_BASE = 10000.0
ROPE_FREQS = DA_HEAD_DIM // 4
GM_CHUNK = 128
GM_HALF = 2 * D_MODEL
GM_GROUPS = 8
GM_GROUP_DIM = GM_HALF // GM_GROUPS
FFN_DIM = 2816
N_EXPERTS = 8
TOP_K = 2
EXPERT_DIM = 3584
EPS = 1e-6
N_ATTN_LAYERS = (DEPTH + 1) // 2
N_GMLP_LAYERS = DEPTH // 2

kernel_name = 'hybrid_diffattn_chunkgmlp_moe_dit'


def rmsnorm(x, w):
    xf = x.astype(jnp.float32)
    y = xf * lax.rsqrt(jnp.mean(xf * xf, axis=-1, keepdims=True) + EPS)
    return (y * w).astype(x.dtype)


def layernorm(x, w, b):
    xf = x.astype(jnp.float32)
    mu = jnp.mean(xf, axis=-1, keepdims=True)
    xc = xf - mu
    y = xc * lax.rsqrt(jnp.mean(xc * xc, axis=-1, keepdims=True) + EPS)
    return (y * w + b).astype(x.dtype)


def ada(cond, w_mod, b_mod):
    m = jax.nn.silu(cond) @ w_mod + b_mod
    return jnp.split(m, 6, axis=-1)


def modulate(h, shift, scale):
    return h * (1.0 + scale) + shift


def rope_axis(x, ang):
    x1, x2 = jnp.split(x, 2, axis=-1)
    cs = jnp.cos(ang)[None, :, None, None, :]
    sn = jnp.sin(ang)[None, :, None, None, :]
    return jnp.concatenate([x1 * cs - x2 * sn, x1 * sn + x2 * cs], axis=-1).astype(x.dtype)


def rope2d(x, ang_row, ang_col):
    xr, xcl = jnp.split(x, 2, axis=-1)
    return jnp.concatenate([rope_axis(xr, ang_row), rope_axis(xcl, ang_col)], axis=-1)


def diff_attention(h, hc, w_qkv, w_o, lq1, lk1, lq2, lk2, subln_w, lam_init, ang_row, ang_col, ctx_out):
    B, S, _ = h.shape
    L = hc.shape[1]
    qkv = h @ w_qkv
    q = qkv[..., :DA_QK_WIDTH].reshape(B, S, DA_HEADS, 2, DA_HEAD_DIM)
    k = qkv[..., DA_QK_WIDTH:2 * DA_QK_WIDTH].reshape(B, S, DA_HEADS, 2, DA_HEAD_DIM)
    v = qkv[..., 2 * DA_QK_WIDTH:].reshape(B, S, DA_HEADS, DA_V_DIM)
    q = rope2d(q, ang_row, ang_col)
    k = rope2d(k, ang_row, ang_col)
    if ctx_out:
        qkv_c = hc @ w_qkv
        qc = qkv_c[..., :DA_QK_WIDTH].reshape(B, L, DA_HEADS, 2, DA_HEAD_DIM)
        kv_c = qkv_c[..., DA_QK_WIDTH:]
    else:
        kv_c = hc @ w_qkv[:, DA_QK_WIDTH:]
    kc = kv_c[..., :DA_QK_WIDTH].reshape(B, L, DA_HEADS, 2, DA_HEAD_DIM)
    vc = kv_c[..., DA_QK_WIDTH:].reshape(B, L, DA_HEADS, DA_V_DIM)
    lam = (jnp.exp(jnp.sum(lq1.astype(jnp.float32) * lk1.astype(jnp.float32)))
           - jnp.exp(jnp.sum(lq2.astype(jnp.float32) * lk2.astype(jnp.float32))) + lam_init)
    scale = DA_HEAD_DIM ** -0.5

    def attend(qb, keys, vals):
        s = jnp.einsum('bqhcd,bkhcd->bhcqk', qb, keys).astype(jnp.float32) * scale
        p = jax.nn.softmax(s, axis=-1)
        a = p[:, :, 0] - lam * p[:, :, 1]
        return jnp.einsum('bhqk,bkhv->bqhv', a.astype(vals.dtype), vals)

    k_all = jnp.concatenate([kc, k], axis=1)
    v_all = jnp.concatenate([vc, v], axis=1)
    n_blk = S // Q_BLOCK
    qb = q.reshape(B, n_blk, Q_BLOCK, DA_HEADS, 2, DA_HEAD_DIM).swapaxes(0, 1)
    o = lax.map(lambda blk: attend(blk, k_all, v_all), qb)
    o = o.swapaxes(0, 1).reshape(B, S, DA_HEADS, DA_V_DIM)

    def finish(out):
        out = rmsnorm(out, subln_w) * (1.0 - lam_init)
        return out.reshape(out.shape[0], out.shape[1], DA_V_WIDTH) @ w_o

    y = finish(o)
    yc = finish(attend(qc, kc, vc)) if ctx_out else None
    return y, yc


def chunk_gmlp(h, w_in, vn_w, vn_b, w_s, b_s, w_out):
    B, L, _ = h.shape
    z = jax.nn.gelu(h @ w_in, approximate=False)
    u, v = z[..., :GM_HALF], z[..., GM_HALF:]
    v = layernorm(v, vn_w, vn_b)
    v = v.reshape(B, L // GM_CHUNK, GM_CHUNK, GM_GROUPS, GM_GROUP_DIM)
    s = jnp.einsum('gpq,bnqgc->bnpgc', w_s, v) + b_s.T[:, :, None]
    return (u * s.reshape(B, L, GM_HALF)) @ w_out


def swiglu(h, wg, wu, wd):
    return (jax.nn.silu(h @ wg) * (h @ wu)) @ wd


def moe_ffn(h, w_router, wg, wu, wd):
    logits = (h @ w_router).astype(jnp.float32)
    top_v, top_i = lax.top_k(logits, TOP_K)
    gates = jax.nn.softmax(top_v, axis=-1)
    y = jnp.zeros_like(h)
    for e in range(N_EXPERTS):
        g_e = jnp.sum(jnp.where(top_i == e, gates, 0.0), axis=-1, keepdims=True).astype(h.dtype)
        y = y + g_e * swiglu(h, wg[e], wu[e], wd[e])
    return y


def setup_inputs(seed: int = 0) -> dict:
    key = jax.random.key(seed)
    ks = jax.random.split(key, 32)
    f32 = jnp.float32
    D = D_MODEL

    def nrm(k, shape, s):
        return jax.random.normal(k, shape, f32) * s

    def gain(k, shape):
        return 1.0 + 0.02 * jax.random.normal(k, shape, f32)

    return {
        'x': nrm(ks[0], (BATCH, SEQ, D), 1.0),
        'c': nrm(ks[1], (BATCH, D), 1.0),
        'ctx': nrm(ks[2], (BATCH, CTX_LEN, D), 1.0),
        'c_ctx': nrm(ks[3], (D,), 1.0),
        'w_mod': nrm(ks[4], (DEPTH, D, 6 * D), 0.5 * D ** -0.5),
        'b_mod': nrm(ks[5], (DEPTH, 6 * D), 0.02),
        'norm1_w': gain(ks[6], (DEPTH, D)),
        'norm2_w': gain(ks[7], (DEPTH, D)),
        'final_norm_w': gain(ks[8], (D,)),
        'da_w_qkv': nrm(ks[9], (N_ATTN_LAYERS, D, 2 * DA_QK_WIDTH + DA_V_WIDTH), D ** -0.5),
        'da_w_o': nrm(ks[10], (N_ATTN_LAYERS, DA_V_WIDTH, D), DA_V_WIDTH ** -0.5),
        'da_lambda_q1': nrm(ks[11], (N_ATTN_LAYERS, DA_HEAD_DIM), 0.1),
        'da_lambda_k1': nrm(ks[12], (N_ATTN_LAYERS, DA_HEAD_DIM), 0.1),
        'da_lambda_q2': nrm(ks[13], (N_ATTN_LAYERS, DA_HEAD_DIM), 0.1),
        'da_lambda_k2': nrm(ks[14], (N_ATTN_LAYERS, DA_HEAD_DIM), 0.1),
        'da_subln_w': gain(ks[15], (N_ATTN_LAYERS, DA_V_DIM)),
        'gm_w_in': nrm(ks[16], (N_GMLP_LAYERS, D, 2 * GM_HALF), D ** -0.5),
        'gm_vnorm_w': gain(ks[17], (N_GMLP_LAYERS, GM_HALF)),
        'gm_vnorm_b': nrm(ks[18], (N_GMLP_LAYERS, GM_HALF), 0.02),
        'gm_w_s': nrm(ks[19], (N_GMLP_LAYERS, GM_GROUPS, GM_CHUNK, GM_CHUNK), GM_CHUNK ** -0.5),
        'gm_b_s': gain(ks[20], (N_GMLP_LAYERS, GM_GROUPS, GM_CHUNK)),
        'gm_w_out': nrm(ks[21], (N_GMLP_LAYERS, GM_HALF, D), GM_HALF ** -0.5),
        'ffn_w_gate': nrm(ks[22], (N_ATTN_LAYERS, D, FFN_DIM), D ** -0.5),
        'ffn_w_up': nrm(ks[23], (N_ATTN_LAYERS, D, FFN_DIM), D ** -0.5),
        'ffn_w_down': nrm(ks[24], (N_ATTN_LAYERS, FFN_DIM, D), FFN_DIM ** -0.5),
        'moe_w_router': nrm(ks[25], (N_GMLP_LAYERS, D, N_EXPERTS), D ** -0.5),
        'moe_w_gate': nrm(ks[26], (N_GMLP_LAYERS, N_EXPERTS, D, EXPERT_DIM), D ** -0.5),
        'moe_w_up': nrm(ks[27], (N_GMLP_LAYERS, N_EXPERTS, D, EXPERT_DIM), D ** -0.5),
        'moe_w_down': nrm(ks[28], (N_GMLP_LAYERS, N_EXPERTS, EXPERT_DIM, D), EXPERT_DIM ** -0.5),
    }


def reference(x, c, ctx, c_ctx, w_mod, b_mod, norm1_w, norm2_w, final_norm_w,
              da_w_qkv, da_w_o, da_lambda_q1, da_lambda_k1, da_lambda_q2, da_lambda_k2, da_subln_w,
              gm_w_in, gm_vnorm_w, gm_vnorm_b, gm_w_s, gm_b_s, gm_w_out,
              ffn_w_gate, ffn_w_up, ffn_w_down,
              moe_w_router, moe_w_gate, moe_w_up, moe_w_down):
    n_tok = x.shape[1]
    rows = n_tok // GRID_W
    row = jnp.broadcast_to(jnp.arange(rows, dtype=jnp.float32)[:, None], (rows, GRID_W)).reshape(-1)
    col = jnp.broadcast_to(jnp.arange(GRID_W, dtype=jnp.float32)[None, :], (rows, GRID_W)).reshape(-1)
    inv_freq = ROPE_BASE ** (-jnp.arange(ROPE_FREQS, dtype=jnp.float32) / ROPE_FREQS)
    ang_row = row[:, None] * inv_freq
    ang_col = col[:, None] * inv_freq

    xc = ctx
    for i in range(DEPTH):
        j = i // 2
        is_attn = (i % N_MIXERS == 0)
        ctx_out = any(l % N_MIXERS == 0 for l in range(i + 1, DEPTH))
        sh1, sc1, g1, sh2, sc2, g2 = ada(c[:, None, :], w_mod[i], b_mod[i])
        csh1, csc1, cg1, csh2, csc2, cg2 = ada(c_ctx, w_mod[i], b_mod[i])
        h = modulate(rmsnorm(x, norm1_w[i]), sh1, sc1)
        if is_attn or ctx_out:
            hc = modulate(rmsnorm(xc, norm1_w[i]), csh1, csc1)
        if is_attn:
            lam_init = 0.8 - 0.6 * math.exp(-0.3 * i)
            y, yc = diff_attention(h, hc, da_w_qkv[j], da_w_o[j], da_lambda_q1[j], da_lambda_k1[j],
                                   da_lambda_q2[j], da_lambda_k2[j], da_subln_w[j], lam_init,
                                   ang_row, ang_col, ctx_out)
        else:
            gm = (gm_w_in[j], gm_vnorm_w[j], gm_vnorm_b[j], gm_w_s[j], gm_b_s[j], gm_w_out[j])
            y = chunk_gmlp(h, *gm)
            yc = chunk_gmlp(hc, *gm) if ctx_out else None
        x = x + g1 * y
        if ctx_out:
            xc = xc + cg1 * yc

        if i % 2 == 0:
            ffn = lambda t: swiglu(t, ffn_w_gate[j], ffn_w_up[j], ffn_w_down[j])
        else:
            ffn = lambda t: moe_ffn(t, moe_w_router[j], moe_w_gate[j], moe_w_up[j], moe_w_down[j])
        h = modulate(rmsnorm(x, norm2_w[i]), sh2, sc2)
        x = x + g2 * ffn(h)
        if ctx_out:
            hc = modulate(rmsnorm(xc, norm2_w[i]), csh2, csc2)
            xc = xc + cg2 * ffn(hc)
    return rmsnorm(x, final_norm_w)
```

```python
import functools
import math

import jax
import jax.numpy as jnp
from jax import lax
from jax.experimental import pallas as pl
from jax.experimental.pallas import tpu as pltpu

F32 = jnp.float32
BF16 = jnp.bfloat16

EPS = 1e-6
GRID_W = 64
ROPE_BASE = 10000.0
N_HEADS = 8
HEAD_DIM = 64
ROPE_FREQS = HEAD_DIM // 4
V_DIM = 2 * HEAD_DIM
QK_WIDTH = N_HEADS * 2 * HEAD_DIM
V_WIDTH = N_HEADS * V_DIM
GM_CHUNK = 128
GM_GROUPS = 8
TOP_K = 2
NEG_BIG = -1e30

LANES = 128
VMEM_LIMIT_BYTES = 56 * 1024 * 1024


def _cparams(semantics):
    return pltpu.CompilerParams(dimension_semantics=semantics, vmem_limit_bytes=VMEM_LIMIT_BYTES)


def _rms(x, w):
    return x * lax.rsqrt(jnp.mean(x * x, axis=-1, keepdims=True) + EPS) * w


def _silu(x):
    return x * jax.nn.sigmoid(x)


def _const_spec(shape):
    nd = len(shape)
    return pl.BlockSpec(shape, lambda *_: (0,) * nd, pipeline_mode=pl.Buffered(1))


def _ada_kernel(cond_ref, w_ref, b_ref, o_ref):
    s = _silu(cond_ref[...])
    o_ref[0] = jnp.dot(s, w_ref[0], preferred_element_type=F32, precision=lax.Precision.HIGHEST) + b_ref[0]


def _ada(cond, w_mod, b_mod):
    depth, d, n = w_mod.shape
    rows = cond.shape[0]
    tn = n // 4
    return pl.pallas_call(
        _ada_kernel,
        out_shape=jax.ShapeDtypeStruct((depth, rows, n), F32),
        grid=(depth, n // tn),
        in_specs=[
            pl.BlockSpec((rows, d), lambda l, j: (0, 0)),
            pl.BlockSpec((1, d, tn), lambda l, j: (l, 0, j)),
            pl.BlockSpec((1, 1, tn), lambda l, j: (l, 0, j)),
        ],
        out_specs=pl.BlockSpec((1, rows, tn), lambda l, j: (l, 0, j)),
        compiler_params=_cparams(("arbitrary", "arbitrary")),
        name="ada",
    )(cond, w_mod, b_mod.reshape(depth, 1, n))


def _qkv_kernel(x_ref, ctx_ref, nw_ref, sh_ref, sc_ref, csh_ref, csc_ref, wqt_ref, wk_ref, wvt_ref,
                ck_ref, sk_ref, cq_ref, sq_ref, k_ref, qt_ref, vt_ref, *, n_ctx, ts):
    is_ctx = pl.program_id(1) < n_ctx
    xin = jnp.where(is_ctx, ctx_ref[0], x_ref[0])
    sh = jnp.where(is_ctx, csh_ref[0], sh_ref[0])
    sc = jnp.where(is_ctx, csc_ref[0], sc_ref[0])
    h = (_rms(xin, nw_ref[...]) * (1.0 + sc) + sh).astype(BF16)

    k = jnp.dot(h, wk_ref[...], preferred_element_type=F32)
    lane = lax.broadcasted_iota(jnp.int32, (ts, LANES), 1)
    lower = (lane % 32) < 16
    ck = ck_ref[...]
    sk = sk_ref[...]
    for j in range(QK_WIDTH // LANES):
        kb = k[:, j * LANES:(j + 1) * LANES]
        partner = jnp.where(lower, pltpu.roll(kb, LANES - 16, 1), pltpu.roll(kb, 16, 1))
        k_ref[0, :, j * LANES:(j + 1) * LANES] = (kb * ck + partner * sk).astype(BF16)

    nt = (((1,), (1,)), ((), ()))
    qt = lax.dot_general(wqt_ref[...], h, nt, preferred_element_type=F32)
    cq = cq_ref[...]
    sq = sq_ref[...]
    scale = HEAD_DIM ** -0.5
    for g in range(QK_WIDTH // HEAD_DIM):
        blk = qt[g * HEAD_DIM:(g + 1) * HEAD_DIM]
        partner = jnp.concatenate([blk[16:32], blk[0:16], blk[48:64], blk[32:48]], axis=0)
        qt_ref[0, g * HEAD_DIM:(g + 1) * HEAD_DIM, :] = ((blk * cq + partner * sq) * scale).astype(BF16)

    vt = lax.dot_general(wvt_ref[...], h, nt, preferred_element_type=F32)
    for hh in range(N_HEADS):
        vt_ref[0, hh, 0] = vt[hh * V_DIM:(hh + 1) * V_DIM].astype(BF16)


def _rope_tables(s_len, ctx_len):
    pos = jnp.arange(s_len)
    row = (pos // GRID_W).astype(F32)
    col = (pos % GRID_W).astype(F32)
    inv_freq = ROPE_BASE ** (-jnp.arange(ROPE_FREQS, dtype=F32) / ROPE_FREQS)
    ar = row[:, None] * inv_freq
    ac = col[:, None] * inv_freq
    cos64 = jnp.concatenate([jnp.cos(ar), jnp.cos(ar), jnp.cos(ac), jnp.cos(ac)], axis=-1)
    sin64 = jnp.concatenate([-jnp.sin(ar), jnp.sin(ar), -jnp.sin(ac), jnp.sin(ac)], axis=-1)
    ck = jnp.concatenate([jnp.ones((ctx_len, LANES), F32), jnp.tile(cos64, (1, 2))], axis=0)
    sk = jnp.concatenate([jnp.zeros((ctx_len, LANES), F32), jnp.tile(sin64, (1, 2))], axis=0)
    return ck, sk, cos64.T, sin64.T


def _qkv(x, ctx, nw, sh, sc, csh, csc, wqt, wk, wvt, ts):
    b, s_len, d = x.shape
    ctx_len = ctx.shape[1]
    assert s_len % ts == 0 and ctx_len % ts == 0
    n_ctx = ctx_len // ts
    n_lat = s_len // ts
    ktot = ctx_len + s_len
    ck, sk, cq, sq = _rope_tables(s_len, ctx_len)

    def lat(t):
        return jnp.maximum(t - n_ctx, 0)

    row = lambda bb, t: (bb, 0, 0)
    return pl.pallas_call(
        functools.partial(_qkv_kernel, n_ctx=n_ctx, ts=ts),
        out_shape=(
            jax.ShapeDtypeStruct((b, ktot, QK_WIDTH), BF16),
            jax.ShapeDtypeStruct((b, QK_WIDTH, s_len), BF16),
            jax.ShapeDtypeStruct((b, N_HEADS, ktot // ts, V_DIM, ts), BF16),
        ),
        grid=(b, n_ctx + n_lat),
        in_specs=[
            pl.BlockSpec((1, ts, d), lambda bb, t: (bb, lat(t), 0)),
            pl.BlockSpec((1, ts, d), lambda bb, t: (bb, jnp.minimum(t, n_ctx - 1), 0)),
            _const_spec((1, d)),
            pl.BlockSpec((1, 1, d), row),
            pl.BlockSpec((1, 1, d), row),
            _const_spec((1, 1, d)),
            _const_spec((1, 1, d)),
            _const_spec((QK_WIDTH, d)),
            _const_spec((d, QK_WIDTH)),
            _const_spec((V_WIDTH, d)),
            pl.BlockSpec((ts, LANES), lambda bb, t: (t, 0)),
            pl.BlockSpec((ts, LANES), lambda bb, t: (t, 0)),
            pl.BlockSpec((HEAD_DIM, ts), lambda bb, t: (0, lat(t))),
            pl.BlockSpec((HEAD_DIM, ts), lambda bb, t: (0, lat(t))),
        ],
        out_specs=(
            pl.BlockSpec((1, ts, QK_WIDTH), lambda bb, t: (bb, t, 0)),
            pl.BlockSpec((1, QK_WIDTH, ts), lambda bb, t: (bb, 0, lat(t))),
            pl.BlockSpec((1, N_HEADS, 1, V_DIM, ts), lambda bb, t: (bb, 0, t, 0, 0)),
        ),
        compiler_params=_cparams(("arbitrary", "arbitrary")),
        name="qkv",
    )(x, ctx, nw, sh, sc, csh, csc, wqt, wk, wvt, ck, sk, cq, sq)


def _attn_kernel(lq1_ref, lk1_ref, lq2_ref, lk2_ref, subw_ref, q_ref, k_ref, v_ref, o_ref, w_scr, acc_scr,
                 *, tq, tk, nk, lam_init):
    q = q_ref[0]
    zero = jnp.zeros((HEAD_DIM, tq), BF16)
    w_scr[0:HEAD_DIM, 0:tq] = q[0:HEAD_DIM]
    w_scr[0:HEAD_DIM, tq:2 * tq] = zero
    w_scr[HEAD_DIM:2 * HEAD_DIM, 0:tq] = zero
    w_scr[HEAD_DIM:2 * HEAD_DIM, tq:2 * tq] = q[HEAD_DIM:2 * HEAD_DIM]
    acc_scr[...] = jnp.zeros_like(acc_scr)

    def body(c, carry):
        m, l = carry
        off = pl.multiple_of(c * tk, tk)
        s = jnp.dot(k_ref[0, pl.ds(off, tk), :], w_scr[...], preferred_element_type=F32)
        m_new = jnp.maximum(m, jnp.max(s, axis=0, keepdims=True))
        alpha = jnp.exp(m - m_new)
        p = jnp.exp(s - m_new)
        l_new = alpha * l + jnp.sum(p, axis=0, keepdims=True)
        pv = jnp.dot(v_ref[0, 0, c], p.astype(BF16), preferred_element_type=F32)
        acc_scr[...] = acc_scr[...] * alpha + pv
        return m_new, l_new

    m0 = jnp.full((1, 2 * tq), NEG_BIG, F32)
    l0 = jnp.zeros((1, 2 * tq), F32)
    _, l = lax.fori_loop(0, nk, body, (m0, l0))

    lam = (jnp.exp(jnp.sum(lq1_ref[...] * lk1_ref[...], axis=-1, keepdims=True))
           - jnp.exp(jnp.sum(lq2_ref[...] * lk2_ref[...], axis=-1, keepdims=True)) + lam_init)
    acc = acc_scr[...]
    inv = 1.0 / l
    o = acc[:, 0:tq] * inv[:, 0:tq] - lam * (acc[:, tq:2 * tq] * inv[:, tq:2 * tq])
    y = o * lax.rsqrt(jnp.mean(o * o, axis=0, keepdims=True) + EPS) * subw_ref[...]
    y = y * (1.0 - lam_init)
    o_ref[0] = y.T.astype(BF16)


def _attention(qt, k_all, vt5, lq1, lk1, lq2, lk2, subw, lam_init, tq):
    b, _, s_len = qt.shape
    ktot = k_all.shape[1]
    nk, tk = vt5.shape[2], vt5.shape[4]
    assert s_len % tq == 0 and nk * tk == ktot
    vec = _const_spec((1, HEAD_DIM))
    return pl.pallas_call(
        functools.partial(_attn_kernel, tq=tq, tk=tk, nk=nk, lam_init=lam_init),
        out_shape=jax.ShapeDtypeStruct((b, s_len, V_WIDTH), BF16),
        grid=(b, N_HEADS, s_len // tq),
        in_specs=[
            vec, vec, vec, vec,
            _const_spec((V_DIM, 1)),
            pl.BlockSpec((1, 2 * HEAD_DIM, tq), lambda bb, hh, i: (bb, hh, i)),
            pl.BlockSpec((1, ktot, 2 * HEAD_DIM), lambda bb, hh, i: (bb, 0, hh)),
            pl.BlockSpec((1, 1, nk, V_DIM, tk), lambda bb, hh, i: (bb, hh, 0, 0, 0)),
        ],
        out_specs=pl.BlockSpec((1, tq, V_DIM), lambda bb, hh, i: (bb, i, hh)),
        scratch_shapes=[
            pltpu.VMEM((2 * HEAD_DIM, 2 * tq), BF16),
            pltpu.VMEM((V_DIM, 2 * tq), F32),
        ],
        compiler_params=_cparams(("arbitrary", "arbitrary", "arbitrary")),
        name="attention",
    )(lq1, lk1, lq2, lk2, subw, qt, k_all, vt5)


def _ffn_kernel(x_ref, o_ref, wo_ref, g1_ref, nw_ref, sh_ref, sc_ref, g2_ref, wg_ref, wu_ref, wd_ref, out_ref):
    y = jnp.dot(o_ref[0], wo_ref[...], preferred_element_type=F32)
    x1 = x_ref[0] + g1_ref[0] * y
    h = (_rms(x1, nw_ref[...]) * (1.0 + sc_ref[0]) + sh_ref[0]).astype(BF16)
    a = jnp.dot(h, wg_ref[...], preferred_element_type=F32)
    u = jnp.dot(h, wu_ref[...], preferred_element_type=F32)
    z = (_silu(a) * u).astype(BF16)
    out_ref[0] = x1 + g2_ref[0] * jnp.dot(z, wd_ref[...], preferred_element_type=F32)


def _ffn(x, o, wo, g1, nw, sh, sc, g2, wg, wu, wd, ts):
    b, s_len, d = x.shape
    f = wg.shape[1]
    row = lambda bb, t: (bb, 0, 0)
    tile = lambda bb, t: (bb, t, 0)
    vec = pl.BlockSpec((1, 1, d), row)
    return pl.pallas_call(
        _ffn_kernel,
        out_shape=jax.ShapeDtypeStruct((b, s_len, d), F32),
        grid=(b, s_len // ts),
        in_specs=[
            pl.BlockSpec((1, ts, d), tile),
            pl.BlockSpec((1, ts, V_WIDTH), tile),
            _const_spec((V_WIDTH, d)),
            vec, _const_spec((1, d)), vec, vec, vec,
            _const_spec((d, f)), _const_spec((d, f)), _const_spec((f, d)),
        ],
        out_specs=pl.BlockSpec((1, ts, d), tile),
        compiler_params=_cparams(("arbitrary", "arbitrary")),
        name="ffn",
    )(x, o, wo, g1, nw, sh, sc, g2, wg, wu, wd)


def _gmlp_kernel(x_ref, nw1_ref, sh1_ref, sc1_ref, g1_ref, win_ref, vnw_ref, vnb_ref, ws_ref, bs_ref, wout_ref,
                 nw2_ref, sh2_ref, sc2_ref, wrt_ref, x3_ref, h3_ref, gate_ref, t_scr, *, ts, half):
    x2 = x_ref[0]
    h = (_rms(x2, nw1_ref[...]) * (1.0 + sc1_ref[0]) + sh1_ref[0]).astype(BF16)
    z = jnp.dot(h, win_ref[...], preferred_element_type=F32)
    z = 0.5 * z * (1.0 + lax.erf(z * (2.0 ** -0.5)))
    u = z[:, 0:half]
    v = z[:, half:2 * half]
    mu = jnp.mean(v, axis=-1, keepdims=True)
    vc = v - mu
    v = vc * lax.rsqrt(jnp.mean(vc * vc, axis=-1, keepdims=True) + EPS) * vnw_ref[...] + vnb_ref[...]
    vb = v.astype(BF16)
    gd = half // GM_GROUPS
    for n in range(ts // GM_CHUNK):
        rows = slice(n * GM_CHUNK, (n + 1) * GM_CHUNK)
        for g in range(GM_GROUPS):
            cols = slice(g * gd, (g + 1) * gd)
            s = jnp.dot(ws_ref[g], vb[rows, cols], preferred_element_type=F32) + bs_ref[g]
            t_scr[rows, cols] = (u[rows, cols] * s).astype(BF16)
    x3 = x2 + g1_ref[0] * jnp.dot(t_scr[...], wout_ref[...], preferred_element_type=F32)
    x3_ref[0] = x3

    h3 = _rms(x3, nw2_ref[...]) * (1.0 + sc2_ref[0]) + sh2_ref[0]
    h3_ref[0] = h3.astype(BF16)
    nt = (((1,), (1,)), ((), ()))
    logits = lax.dot_general(wrt_ref[...], h3, nt, preferred_element_type=F32,
                             precision=lax.Precision.HIGHEST)
    ne = logits.shape[0]
    eid = lax.broadcasted_iota(jnp.int32, logits.shape, 0)
    m1 = jnp.max(logits, axis=0, keepdims=True)
    i1 = jnp.min(jnp.where(logits == m1, eid, ne), axis=0, keepdims=True)
    rest = jnp.where(eid == i1, -jnp.inf, logits)
    m2 = jnp.max(rest, axis=0, keepdims=True)
    i2 = jnp.min(jnp.where(rest == m2, eid, ne), axis=0, keepdims=True)
    e2 = jnp.exp(m2 - m1)
    den = 1.0 + e2
    gate_ref[0] = jnp.where(eid == i1, 1.0 / den, jnp.where(eid == i2, e2 / den, 0.0))


def _gmlp(x, nw1, sh1, sc1, g1, win, vnw, vnb, ws, bs, wout, nw2, sh2, sc2, wrt, ts):
    b, s_len, d = x.shape
    half = wout.shape[0]
    ne = wrt.shape[0]
    assert ts % GM_CHUNK == 0 and s_len % ts == 0
    row = lambda bb, t: (bb, 0, 0)
    tile = lambda bb, t: (bb, t, 0)
    vec = pl.BlockSpec((1, 1, d), row)
    return pl.pallas_call(
        functools.partial(_gmlp_kernel, ts=ts, half=half),
        out_shape=(
            jax.ShapeDtypeStruct((b, s_len, d), F32),
            jax.ShapeDtypeStruct((b, s_len, d), BF16),
            jax.ShapeDtypeStruct((b, ne, s_len), F32),
        ),
        grid=(b, s_len // ts),
        in_specs=[
            pl.BlockSpec((1, ts, d), tile),
            _const_spec((1, d)), vec, vec, vec,
            _const_spec((d, 2 * half)),
            _const_spec((1, half)), _const_spec((1, half)),
            _const_spec((GM_GROUPS, GM_CHUNK, GM_CHUNK)),
            _const_spec((GM_GROUPS, GM_CHUNK, 1)),
            _const_spec((half, d)),
            _const_spec((1, d)), vec, vec,
            _const_spec((ne, d)),
        ],
        out_specs=(
            pl.BlockSpec((1, ts, d), tile),
            pl.BlockSpec((1, ts, d), tile),
            pl.BlockSpec((1, ne, ts), lambda bb, t: (bb, 0, t)),
        ),
        scratch_shapes=[pltpu.VMEM((ts, half), BF16)],
        compiler_params=_cparams(("arbitrary", "arbitrary")),
        name="gmlp",
    )(x, nw1, sh1, sc1, g1, win, vnw, vnb, ws, bs, wout, nw2, sh2, sc2, wrt)


def _moe_kernel(x_ref, h_ref, gate_ref, g2_ref, fw_ref, wg_ref, wu_ref, wd_ref, out_ref, acc_scr):
    e = pl.program_id(2)
    f = pl.program_id(3)

    @pl.when((e == 0) & (f == 0))
    def _():
        acc_scr[...] = jnp.zeros_like(acc_scr)

    h = h_ref[0]
    gates = gate_ref[0]
    lane = lax.broadcasted_iota(jnp.int32, gates.shape, 1)
    g = jnp.sum(jnp.where(lane == e, gates, 0.0), axis=-1, keepdims=True)
    a = jnp.dot(h, wg_ref[0], preferred_element_type=F32)
    u = jnp.dot(h, wu_ref[0], preferred_element_type=F32)
    z = (_silu(a) * u * g).astype(BF16)
    acc_scr[...] += jnp.dot(z, wd_ref[0], preferred_element_type=F32)

    @pl.when((e == pl.num_programs(2) - 1) & (f == pl.num_programs(3) - 1))
    def _():
        out_ref[0] = _rms(x_ref[0] + g2_ref[0] * acc_scr[...], fw_ref[...])


def _moe(x, h, gates, g2, fw, wg, wu, wd, tm, tf):
    b, s_len, d = x.shape
    ne, _, f = wg.shape
    assert s_len % tm == 0 and f % tf == 0
    tile = lambda bb, t, e, j: (bb, t, 0)
    return pl.pallas_call(
        _moe_kernel,
        out_shape=jax.ShapeDtypeStruct((b, s_len, d), F32),
        grid=(b, s_len // tm, ne, f // tf),
        in_specs=[
            pl.BlockSpec((1, tm, d), tile),
            pl.BlockSpec((1, tm, d), tile),
            pl.BlockSpec((1, tm, ne), tile),
            pl.BlockSpec((1, 1, d), lambda bb, t, e, j: (bb, 0, 0)),
            pl.BlockSpec((1, d), lambda bb, t, e, j: (0, 0)),
            pl.BlockSpec((1, d, tf), lambda bb, t, e, j: (e, 0, j)),
            pl.BlockSpec((1, d, tf), lambda bb, t, e, j: (e, 0, j)),
            pl.BlockSpec((1, tf, d), lambda bb, t, e, j: (e, j, 0)),
        ],
        out_specs=pl.BlockSpec((1, tm, d), tile),
        scratch_shapes=[pltpu.VMEM((tm, d), F32)],
        compiler_params=_cparams(("arbitrary", "arbitrary", "arbitrary", "arbitrary")),
        name="moe",
    )(x, h, gates, g2, fw, wg, wu, wd)


def _pick(n, prefs):
    for p in prefs:
        if n % p == 0:
            return p
    return n


def kernel(x, c, ctx, c_ctx, w_mod, b_mod, norm1_w, norm2_w, final_norm_w, da_w_qkv, da_w_o, da_lambda_q1, da_lambda_k1, da_lambda_q2, da_lambda_k2, da_subln_w, gm_w_in, gm_vnorm_w, gm_vnorm_b, gm_w_s, gm_b_s, gm_w_out, ffn_w_gate, ffn_w_up, ffn_w_down, moe_w_router, moe_w_gate, moe_w_up, moe_w_down):
    b, s_len, d = x.shape
    ctx_len = ctx.shape[1]

    pad = (-(b + 1)) % 8
    cond = jnp.concatenate([c, c_ctx[None, :], jnp.zeros((pad, d), F32)], axis=0)
    mod = _ada(cond, w_mod, b_mod)

    def chunks(layer, r0, r1):
        m = mod[layer, r0:r1]
        return [m[:, None, i * d:(i + 1) * d] for i in range(6)]

    sh1, sc1, g1, sh2, sc2, g2 = chunks(0, 0, b)
    csh1, csc1 = chunks(0, b, b + 1)[:2]
    sh1b, sc1b, g1b, sh2b, sc2b, g2b = chunks(1, 0, b)

    ts = _pick(ctx_len, (256, 128))
    wq = da_w_qkv[0, :, 0:QK_WIDTH]
    wk = da_w_qkv[0, :, QK_WIDTH:2 * QK_WIDTH]
    wv = da_w_qkv[0, :, 2 * QK_WIDTH:]
    k_all, qt, vt5 = _qkv(x, ctx, norm1_w[0][None], sh1, sc1, csh1, csc1,
                          wq.T.astype(BF16), wk.astype(BF16), wv.T.astype(BF16), ts)
    lam_init = 0.8 - 0.6 * math.exp(-0.3 * 0)
    o = _attention(qt, k_all, vt5, da_lambda_q1[0][None], da_lambda_k1[0][None], da_lambda_q2[0][None],
                   da_lambda_k2[0][None], da_subln_w[0][:, None], lam_init, _pick(s_len, (512, 256, 128)))
    x2 = _ffn(x, o, da_w_o[0].astype(BF16), g1, norm2_w[0][None], sh2, sc2, g2,
              ffn_w_gate[0].astype(BF16), ffn_w_up[0].astype(BF16), ffn_w_down[0].astype(BF16),
              _pick(s_len, (256, 128)))

    x3, h3, gates = _gmlp(x2, norm1_w[1][None], sh1b, sc1b, g1b, gm_w_in[0].astype(BF16),
                          gm_vnorm_w[0][None], gm_vnorm_b[0][None], gm_w_s[0].astype(BF16),
                          gm_b_s[0][:, :, None], gm_w_out[0].astype(BF16),
                          norm2_w[1][None], sh2b, sc2b, moe_w_router[0].T, _pick(s_len, (256, 128)))
    f = moe_w_gate.shape[-1]
    return _moe(x3, h3, jnp.swapaxes(gates, 1, 2), g2b, final_norm_w[None],
                moe_w_gate[0].astype(BF16), moe_w_up[0].astype(BF16), moe_w_down[0].astype(BF16),
                _pick(s_len, (512, 256, 128)), _pick(f, (512, 256, 128)))
```

```python
import functools
import math

import jax
import jax.numpy as jnp
from jax import lax
from jax.experimental import pallas as pl
from jax.experimental.pallas import tpu as pltpu

F32 = jnp.float32
BF16 = jnp.bfloat16

EPS = 1e-6
GRID_W = 64
ROPE_BASE = 10000.0
N_HEADS = 8
HEAD_DIM = 64
ROPE_FREQS = HEAD_DIM // 4
V_DIM = 2 * HEAD_DIM
QK_WIDTH = N_HEADS * 2 * HEAD_DIM
V_WIDTH = N_HEADS * V_DIM
V_ROWS = V_DIM + 16
KEY_TILES_PER_BLOCK = (5, 4, 3, 2)
SCORE_BUFFERS = 2
DMA_ISSUE_UNROLL = 4
GM_CHUNK = 128
GM_GROUPS = 8
TOP_K = 2
NEG_BIG = -1e30
LOG2_E = math.log2(math.e)

LANES = 128
VMEM_LIMIT_BYTES = 56 * 1024 * 1024


def _cparams(semantics):
    return pltpu.CompilerParams(dimension_semantics=semantics, vmem_limit_bytes=VMEM_LIMIT_BYTES)


def _rms(x, w):
    return x * lax.rsqrt(jnp.mean(x * x, axis=-1, keepdims=True) + EPS) * w


def _silu(x):
    return x * jax.nn.sigmoid(x)


def _const_spec(shape):
    nd = len(shape)
    return pl.BlockSpec(shape, lambda *_: (0,) * nd, pipeline_mode=pl.Buffered(1))


def _ada_kernel(cond_ref, w_ref, b_ref, o_ref):
    s = _silu(cond_ref[...])
    o_ref[0] = jnp.dot(s, w_ref[0], preferred_element_type=F32, precision=lax.Precision.HIGHEST) + b_ref[0]


def _ada(cond, w_mod, b_mod):
    depth, d, n = w_mod.shape
    rows = cond.shape[0]
    tn = n // 4
    return pl.pallas_call(
        _ada_kernel,
        out_shape=jax.ShapeDtypeStruct((depth, rows, n), F32),
        grid=(depth, n // tn),
        in_specs=[
            pl.BlockSpec((rows, d), lambda l, j: (0, 0)),
            pl.BlockSpec((1, d, tn), lambda l, j: (l, 0, j)),
            pl.BlockSpec((1, 1, tn), lambda l, j: (l, 0, j)),
        ],
        out_specs=pl.BlockSpec((1, rows, tn), lambda l, j: (l, 0, j)),
        compiler_params=_cparams(("arbitrary", "arbitrary")),
        name="ada",
    )(cond, w_mod, b_mod.reshape(depth, 1, n))


def _qkv_kernel(x_ref, ctx_ref, nw_ref, sh_ref, sc_ref, csh_ref, csc_ref, wqt_ref, wk_ref, wvt_ref,
                ck_ref, sk_ref, cq_ref, sq_ref, k_ref, qt_ref, vt_ref, *, n_ctx, ts):
    is_ctx = pl.program_id(1) < n_ctx
    xin = jnp.where(is_ctx, ctx_ref[0], x_ref[0])
    sh = jnp.where(is_ctx, csh_ref[0], sh_ref[0])
    sc = jnp.where(is_ctx, csc_ref[0], sc_ref[0])
    h = (_rms(xin, nw_ref[...]) * (1.0 + sc) + sh).astype(BF16)

    k = jnp.dot(h, wk_ref[...], preferred_element_type=F32)
    lane = lax.broadcasted_iota(jnp.int32, (ts, LANES), 1)
    lower = (lane % 32) < 16
    ck = ck_ref[...]
    sk = sk_ref[...]
    for j in range(QK_WIDTH // LANES):
        kb = k[:, j * LANES:(j + 1) * LANES]
        partner = jnp.where(lower, pltpu.roll(kb, LANES - 16, 1), pltpu.roll(kb, 16, 1))
        k_ref[0, :, j * LANES:(j + 1) * LANES] = (kb * ck + partner * sk).astype(BF16)

    nt = (((1,), (1,)), ((), ()))
    qt = lax.dot_general(wqt_ref[...], h, nt, preferred_element_type=F32)
    cq = cq_ref[...]
    sq = sq_ref[...]
    scale = HEAD_DIM ** -0.5 * LOG2_E
    for g in range(QK_WIDTH // HEAD_DIM):
        blk = qt[g * HEAD_DIM:(g + 1) * HEAD_DIM]
        partner = jnp.concatenate([blk[16:32], blk[0:16], blk[48:64], blk[32:48]], axis=0)
        qt_ref[0, g * HEAD_DIM:(g + 1) * HEAD_DIM, :] = ((blk * cq + partner * sq) * scale).astype(BF16)

    vt = lax.dot_general(wvt_ref[...], h, nt, preferred_element_type=F32)
    ones_row = (lax.broadcasted_iota(jnp.int32, (V_ROWS - V_DIM, ts), 0) == 0).astype(BF16)
    for hh in range(N_HEADS):
        vt_ref[0, hh, 0, 0:V_DIM, :] = vt[hh * V_DIM:(hh + 1) * V_DIM].astype(BF16)
        vt_ref[0, hh, 0, V_DIM:V_ROWS, :] = ones_row


def _rope_tables(s_len, ctx_len):
    pos = jnp.arange(s_len)
    row = (pos // GRID_W).astype(F32)
    col = (pos % GRID_W).astype(F32)
    inv_freq = ROPE_BASE ** (-jnp.arange(ROPE_FREQS, dtype=F32) / ROPE_FREQS)
    ar = row[:, None] * inv_freq
    ac = col[:, None] * inv_freq
    cos64 = jnp.concatenate([jnp.cos(ar), jnp.cos(ar), jnp.cos(ac), jnp.cos(ac)], axis=-1)
    sin64 = jnp.concatenate([-jnp.sin(ar), jnp.sin(ar), -jnp.sin(ac), jnp.sin(ac)], axis=-1)
    ck = jnp.concatenate([jnp.ones((ctx_len, LANES), F32), jnp.tile(cos64, (1, 2))], axis=0)
    sk = jnp.concatenate([jnp.zeros((ctx_len, LANES), F32), jnp.tile(sin64, (1, 2))], axis=0)
    return ck, sk, cos64.T, sin64.T


def _qkv(x, ctx, nw, sh, sc, csh, csc, wqt, wk, wvt, ts):
    b, s_len, d = x.shape
    ctx_len = ctx.shape[1]
    assert s_len % ts == 0 and ctx_len % ts == 0
    n_ctx = ctx_len // ts
    n_lat = s_len // ts
    ktot = ctx_len + s_len
    sub = _pick(n_ctx + n_lat, KEY_TILES_PER_BLOCK)
    ck, sk, cq, sq = _rope_tables(s_len, ctx_len)

    def lat(t):
        return jnp.maximum(t - n_ctx, 0)

    row = lambda bb, t: (bb, 0, 0)
    return pl.pallas_call(
        functools.partial(_qkv_kernel, n_ctx=n_ctx, ts=ts),
        out_shape=(
            jax.ShapeDtypeStruct((b, ktot, QK_WIDTH), BF16),
            jax.ShapeDtypeStruct((b, QK_WIDTH, s_len), BF16),
            jax.ShapeDtypeStruct((b, N_HEADS, ktot // (ts * sub), V_ROWS, ts * sub), BF16),
        ),
        grid=(b, n_ctx + n_lat),
        in_specs=[
            pl.BlockSpec((1, ts, d), lambda bb, t: (bb, lat(t), 0)),
            pl.BlockSpec((1, ts, d), lambda bb, t: (bb, jnp.minimum(t, n_ctx - 1), 0)),
            _const_spec((1, d)),
            pl.BlockSpec((1, 1, d), row),
            pl.BlockSpec((1, 1, d), row),
            _const_spec((1, 1, d)),
            _const_spec((1, 1, d)),
            _const_spec((QK_WIDTH, d)),
            _const_spec((d, QK_WIDTH)),
            _const_spec((V_WIDTH, d)),
            pl.BlockSpec((ts, LANES), lambda bb, t: (t, 0)),
            pl.BlockSpec((ts, LANES), lambda bb, t: (t, 0)),
            pl.BlockSpec((HEAD_DIM, ts), lambda bb, t: (0, lat(t))),
            pl.BlockSpec((HEAD_DIM, ts), lambda bb, t: (0, lat(t))),
        ],
        out_specs=(
            pl.BlockSpec((1, ts, QK_WIDTH), lambda bb, t: (bb, t, 0)),
            pl.BlockSpec((1, QK_WIDTH, ts), lambda bb, t: (bb, 0, lat(t))),
            pl.BlockSpec((1, N_HEADS, 1, V_ROWS, ts), lambda bb, t: (bb, 0, t // sub, 0, t % sub)),
        ),
        compiler_params=_cparams(("arbitrary", "arbitrary")),
        name="qkv",
    )(x, ctx, nw, sh, sc, csh, csc, wqt, wk, wvt, ck, sk, cq, sq)


def _attn_kernel(lq1_ref, lk1_ref, lq2_ref, lk2_ref, subw_ref, q_ref, k_ref, v_ref, o_ref, w_scr, acc_scr,
                 *bufs, tq, tkb, nb, lam_init):
    nbuf = len(bufs) // 2
    s_bufs, p_bufs = bufs[:nbuf], bufs[nbuf:]
    q = q_ref[0]
    zero = jnp.zeros((HEAD_DIM, tq), BF16)
    w_scr[0:HEAD_DIM, 0:tq] = q[0:HEAD_DIM]
    w_scr[0:HEAD_DIM, tq:2 * tq] = zero
    w_scr[HEAD_DIM:2 * HEAD_DIM, 0:tq] = zero
    w_scr[HEAD_DIM:2 * HEAD_DIM, tq:2 * tq] = q[HEAD_DIM:2 * HEAD_DIM]
    acc_scr[...] = jnp.zeros_like(acc_scr)

    def scores(blk):
        off = pl.multiple_of(blk * tkb, tkb)
        return jnp.dot(k_ref[0, pl.ds(off, tkb), :], w_scr[...], preferred_element_type=F32)

    def softmax(slot, m):
        s = s_bufs[slot][...]
        m_new = jnp.maximum(m, jnp.max(s, axis=0, keepdims=True))
        p_bufs[slot][...] = jnp.exp2(s - m_new).astype(BF16)
        return m_new, jnp.exp2(m - m_new)

    def values(j, slot, alpha):
        pv = jnp.dot(v_ref[0, 0, j], p_bufs[slot][...], preferred_element_type=F32)
        acc_scr[...] = acc_scr[...] * alpha + pv

    def step(j, slot, m, alpha):
        s_bufs[(slot + 2) % nbuf][...] = scores(j + 2)
        m, alpha_next = softmax((slot + 1) % nbuf, m)
        values(j, slot, alpha)
        return m, alpha_next

    s_bufs[0][...] = scores(0)
    m, alpha = softmax(0, jnp.full((1, 2 * tq), NEG_BIG, F32))
    if nb > 1:
        s_bufs[1][...] = scores(1)
    n_full = max(nb - 2, 0)

    def unrolled(i, carry):
        for r in range(nbuf):
            carry = step(nbuf * i + r, r, *carry)
        return carry

    m, alpha = lax.fori_loop(0, n_full // nbuf, unrolled, (m, alpha))
    for j in range(nbuf * (n_full // nbuf), n_full):
        m, alpha = step(j, j % nbuf, m, alpha)
    if nb > 1:
        m, alpha_next = softmax((nb - 1) % nbuf, m)
        values(nb - 2, (nb - 2) % nbuf, alpha)
        alpha = alpha_next
    values(nb - 1, (nb - 1) % nbuf, alpha)

    lam = (jnp.exp(jnp.sum(lq1_ref[...] * lk1_ref[...], axis=-1, keepdims=True))
           - jnp.exp(jnp.sum(lq2_ref[...] * lk2_ref[...], axis=-1, keepdims=True)) + lam_init)
    acc = acc_scr[0:V_DIM, :]
    inv = 1.0 / acc_scr[V_DIM:V_DIM + 1, :]
    o = acc[:, 0:tq] * inv[:, 0:tq] - lam * (acc[:, tq:2 * tq] * inv[:, tq:2 * tq])
    y = o * lax.rsqrt(jnp.mean(o * o, axis=0, keepdims=True) + EPS) * subw_ref[...]
    y = y * (1.0 - lam_init)
    o_ref[0] = y.T.astype(BF16)


def _attention(qt, k_all, vt5, lq1, lk1, lq2, lk2, subw, lam_init, tq):
    b, _, s_len = qt.shape
    ktot = k_all.shape[1]
    nb, tkb = vt5.shape[2], vt5.shape[4]
    assert s_len % tq == 0 and nb * tkb == ktot and vt5.shape[3] == V_ROWS
    vec = _const_spec((1, HEAD_DIM))
    return pl.pallas_call(
        functools.partial(_attn_kernel, tq=tq, tkb=tkb, nb=nb, lam_init=lam_init),
        out_shape=jax.ShapeDtypeStruct((b, s_len, V_WIDTH), BF16),
        grid=(b, N_HEADS, s_len // tq),
        in_specs=[
            vec, vec, vec, vec,
            _const_spec((V_DIM, 1)),
            pl.BlockSpec((1, 2 * HEAD_DIM, tq), lambda bb, hh, i: (bb, hh, i)),
            pl.BlockSpec((1, ktot, 2 * HEAD_DIM), lambda bb, hh, i: (bb, 0, hh)),
            pl.BlockSpec((1, 1, nb, V_ROWS, tkb), lambda bb, hh, i: (bb, hh, 0, 0, 0)),
        ],
        out_specs=pl.BlockSpec((1, tq, V_DIM), lambda bb, hh, i: (bb, i, hh)),
        scratch_shapes=[
            pltpu.VMEM((2 * HEAD_DIM, 2 * tq), BF16),
            pltpu.VMEM((V_ROWS, 2 * tq), F32),
        ] + [pltpu.VMEM((tkb, 2 * tq), F32)] * SCORE_BUFFERS + [pltpu.VMEM((tkb, 2 * tq), BF16)] * SCORE_BUFFERS + [
        ],
        compiler_params=_cparams(("arbitrary", "arbitrary", "arbitrary")),
        name="attention",
    )(lq1, lk1, lq2, lk2, subw, qt, k_all, vt5)


def _ffn_kernel(x_ref, o_ref, wo_ref, g1_ref, nw_ref, sh_ref, sc_ref, g2_ref, wg_ref, wu_ref, wd_ref, out_ref):
    y = jnp.dot(o_ref[0], wo_ref[...], preferred_element_type=F32)
    x1 = x_ref[0] + g1_ref[0] * y
    h = (_rms(x1, nw_ref[...]) * (1.0 + sc_ref[0]) + sh_ref[0]).astype(BF16)
    a = jnp.dot(h, wg_ref[...], preferred_element_type=F32)
    u = jnp.dot(h, wu_ref[...], preferred_element_type=F32)
    z = (_silu(a) * u).astype(BF16)
    out_ref[0] = x1 + g2_ref[0] * jnp.dot(z, wd_ref[...], preferred_element_type=F32)


def _ffn(x, o, wo, g1, nw, sh, sc, g2, wg, wu, wd, ts):
    b, s_len, d = x.shape
    f = wg.shape[1]
    row = lambda bb, t: (bb, 0, 0)
    tile = lambda bb, t: (bb, t, 0)
    vec = pl.BlockSpec((1, 1, d), row)
    return pl.pallas_call(
        _ffn_kernel,
        out_shape=jax.ShapeDtypeStruct((b, s_len, d), F32),
        grid=(b, s_len // ts),
        in_specs=[
            pl.BlockSpec((1, ts, d), tile),
            pl.BlockSpec((1, ts, V_WIDTH), tile),
            _const_spec((V_WIDTH, d)),
            vec, _const_spec((1, d)), vec, vec, vec,
            _const_spec((d, f)), _const_spec((d, f)), _const_spec((f, d)),
        ],
        out_specs=pl.BlockSpec((1, ts, d), tile),
        compiler_params=_cparams(("arbitrary", "arbitrary")),
        name="ffn",
    )(x, o, wo, g1, nw, sh, sc, g2, wg, wu, wd)


def _gmlp_kernel(x_ref, nw1_ref, sh1_ref, sc1_ref, g1_ref, win_ref, vnw_ref, vnb_ref, ws_ref, bs_ref, wout_ref,
                 nw2_ref, sh2_ref, sc2_ref, wrt_ref, x3_ref, h3_ref, idx_ref, gate_ref, t_scr, *, ts, half):
    x2 = x_ref[0]
    h = (_rms(x2, nw1_ref[...]) * (1.0 + sc1_ref[0]) + sh1_ref[0]).astype(BF16)
    z = jnp.dot(h, win_ref[...], preferred_element_type=F32)
    z = 0.5 * z * (1.0 + lax.erf(z * (2.0 ** -0.5)))
    u = z[:, 0:half]
    v = z[:, half:2 * half]
    mu = jnp.mean(v, axis=-1, keepdims=True)
    vc = v - mu
    v = vc * lax.rsqrt(jnp.mean(vc * vc, axis=-1, keepdims=True) + EPS) * vnw_ref[...] + vnb_ref[...]
    vb = v.astype(BF16)
    gd = half // GM_GROUPS
    for n in range(ts // GM_CHUNK):
        rows = slice(n * GM_CHUNK, (n + 1) * GM_CHUNK)
        for g in range(GM_GROUPS):
            cols = slice(g * gd, (g + 1) * gd)
            s = jnp.dot(ws_ref[g], vb[rows, cols], preferred_element_type=F32) + bs_ref[g]
            t_scr[rows, cols] = (u[rows, cols] * s).astype(BF16)
    x3 = x2 + g1_ref[0] * jnp.dot(t_scr[...], wout_ref[...], preferred_element_type=F32)
    x3_ref[0] = x3

    h3 = _rms(x3, nw2_ref[...]) * (1.0 + sc2_ref[0]) + sh2_ref[0]
    h3_ref[0] = h3
    nt = (((1,), (1,)), ((), ()))
    logits = lax.dot_general(wrt_ref[...], h3, nt, preferred_element_type=F32,
                             precision=lax.Precision.HIGHEST)
    ne = logits.shape[0]
    eid = lax.broadcasted_iota(jnp.int32, logits.shape, 0)
    m1 = jnp.max(logits, axis=0, keepdims=True)
    i1 = jnp.min(jnp.where(logits == m1, eid, ne), axis=0, keepdims=True)
    rest = jnp.where(eid == i1, -jnp.inf, logits)
    m2 = jnp.max(rest, axis=0, keepdims=True)
    i2 = jnp.min(jnp.where(rest == m2, eid, ne), axis=0, keepdims=True)
    e2 = jnp.exp(m2 - m1)
    den = 1.0 + e2
    idx_ref[0] = jnp.concatenate([i1, i2], axis=0)
    gate_ref[0] = jnp.concatenate([1.0 / den, e2 / den], axis=0)


def _gmlp(x, nw1, sh1, sc1, g1, win, vnw, vnb, ws, bs, wout, nw2, sh2, sc2, wrt, ts):
    b, s_len, d = x.shape
    half = wout.shape[0]
    ne = wrt.shape[0]
    assert ts % GM_CHUNK == 0 and s_len % ts == 0
    row = lambda bb, t: (bb, 0, 0)
    tile = lambda bb, t: (bb, t, 0)
    vec = pl.BlockSpec((1, 1, d), row)
    return pl.pallas_call(
        functools.partial(_gmlp_kernel, ts=ts, half=half),
        out_shape=(
            jax.ShapeDtypeStruct((b, s_len, d), F32),
            jax.ShapeDtypeStruct((b, s_len, d), F32),
            jax.ShapeDtypeStruct((b, TOP_K, s_len), jnp.int32),
            jax.ShapeDtypeStruct((b, TOP_K, s_len), F32),
        ),
        grid=(b, s_len // ts),
        in_specs=[
            pl.BlockSpec((1, ts, d), tile),
            _const_spec((1, d)), vec, vec, vec,
            _const_spec((d, 2 * half)),
            _const_spec((1, half)), _const_spec((1, half)),
            _const_spec((GM_GROUPS, GM_CHUNK, GM_CHUNK)),
            _const_spec((GM_GROUPS, GM_CHUNK, 1)),
            _const_spec((half, d)),
            _const_spec((1, d)), vec, vec,
            _const_spec((ne, d)),
        ],
        out_specs=(
            pl.BlockSpec((1, ts, d), tile),
            pl.BlockSpec((1, ts, d), tile),
            pl.BlockSpec((1, TOP_K, ts), lambda bb, t: (bb, 0, t)),
            pl.BlockSpec((1, TOP_K, ts), lambda bb, t: (bb, 0, t)),
        ),
        scratch_shapes=[pltpu.VMEM((ts, half), BF16)],
        compiler_params=_cparams(("arbitrary", "arbitrary")),
        name="gmlp",
    )(x, nw1, sh1, sc1, g1, win, vnw, vnb, ws, bs, wout, nw2, sh2, sc2, wrt)


def _route(idx, n_experts, tm):
    b, k, s_len = idx.shape
    n = b * s_len
    e_flat = jnp.swapaxes(idx, 0, 1).reshape(k * n)
    onehot = (e_flat[:, None] == jnp.arange(n_experts, dtype=jnp.int32)[None, :]).astype(jnp.int32)
    csum = jnp.cumsum(onehot, axis=0)
    counts = csum[-1]
    rank = jnp.take_along_axis(csum, e_flat[:, None], axis=1)[:, 0] - 1
    padded = ((counts + tm - 1) // tm) * tm
    ends = jnp.cumsum(padded)
    starts = ends - padded
    pos = (starts[e_flat] + rank).astype(jnp.int32)
    n_tiles = (k * n) // tm + n_experts
    tile_ids = jnp.arange(n_tiles, dtype=jnp.int32)
    used = tile_ids * tm < ends[-1]
    tile_src = jnp.where(used, tile_ids, ends[-1] // tm - 1).astype(jnp.int32)
    tile_expert = jnp.minimum(jnp.searchsorted(ends, tile_src * tm, side="right"), n_experts - 1).astype(jnp.int32)
    valid_end = (starts + counts)[tile_expert]
    tile_rows = jnp.where(used, jnp.clip(valid_end - tile_src * tm, 0, tm), 0).astype(jnp.int32)
    return pos.reshape(k, n), tile_src, tile_expert, tile_rows


def _dispatch_kernel(pos_ref, h_ref, zeros_ref, xs_ref, sem, *, rows):
    del zeros_ref
    i = pl.program_id(0)

    def row_copy(src_row, dst_row):
        return pltpu.make_async_copy(h_ref.at[pl.ds(src_row, 1)], xs_ref.at[pl.ds(dst_row, 1)], sem)

    def issue(r, carry):
        for k in range(TOP_K):
            row_copy(i * rows + r, pos_ref[0, k, r]).start()
        return carry

    def drain_step():
        n_rows = TOP_K * rows
        pltpu.make_async_copy(h_ref.at[pl.ds(0, n_rows)], xs_ref.at[pl.ds(0, n_rows)], sem).wait()

    lax.fori_loop(0, rows, issue, 0, unroll=DMA_ISSUE_UNROLL)

    @pl.when(i > 0)
    def _():
        drain_step()

    @pl.when(i == pl.num_programs(0) - 1)
    def _():
        drain_step()


def _dispatch(h, pos, n_slots, rows):
    n, d = h.shape
    assert n % rows == 0
    pos3 = jnp.swapaxes(pos.reshape(TOP_K, n // rows, rows), 0, 1)
    return pl.pallas_call(
        functools.partial(_dispatch_kernel, rows=rows),
        out_shape=jax.ShapeDtypeStruct((n_slots, d), F32),
        grid=(n // rows,),
        in_specs=[
            pl.BlockSpec((1, TOP_K, rows), lambda i: (i, 0, 0), memory_space=pltpu.SMEM),
            pl.BlockSpec(memory_space=pl.ANY),
            pl.BlockSpec(memory_space=pl.ANY),
        ],
        out_specs=pl.BlockSpec(memory_space=pl.ANY),
        scratch_shapes=[pltpu.SemaphoreType.DMA(())],
        input_output_aliases={2: 0},
        compiler_params=_cparams(("arbitrary",)),
        name="moe_dispatch",
    )(pos3, h, jnp.zeros((n_slots, d), F32))


def _experts_kernel(src_ref, exp_ref, rows_ref, x_ref, wg_ref, wu_ref, wd_ref, y_ref, xb_scr, acc_scr):
    j = pl.program_id(0)
    f = pl.program_id(1)
    used = rows_ref[j] > 0

    @pl.when(f == 0)
    def _():
        xb_scr[...] = x_ref[...].astype(BF16)
        acc_scr[...] = jnp.zeros_like(acc_scr)

    @pl.when(used)
    def _():
        xb = xb_scr[...]
        a = jnp.dot(xb, wg_ref[0].astype(BF16), preferred_element_type=F32)
        u = jnp.dot(xb, wu_ref[0].astype(BF16), preferred_element_type=F32)
        z = (_silu(a) * u).astype(BF16)
        acc_scr[...] += jnp.dot(z, wd_ref[0].astype(BF16), preferred_element_type=F32)

    @pl.when(f == pl.num_programs(1) - 1)
    def _():
        y_ref[...] = acc_scr[...]


def _experts(xs, tile_src, tile_expert, tile_rows, wg, wu, wd, tm, tf):
    n_slots, d = xs.shape
    ne, _, f = wg.shape
    nf = f // tf
    n_tiles = tile_src.shape[0]
    assert n_slots == n_tiles * tm and f % tf == 0

    def fchunk(j, fi, rows):
        return jnp.where(rows[j] > 0, fi, nf - 1)

    return pl.pallas_call(
        _experts_kernel,
        out_shape=jax.ShapeDtypeStruct((n_slots, d), F32),
        grid_spec=pltpu.PrefetchScalarGridSpec(
            num_scalar_prefetch=3,
            grid=(n_tiles, nf),
            in_specs=[
                pl.BlockSpec((tm, d), lambda j, fi, src, exp, rows: (src[j], 0)),
                pl.BlockSpec((1, d, tf), lambda j, fi, src, exp, rows: (exp[j], 0, fchunk(j, fi, rows))),
                pl.BlockSpec((1, d, tf), lambda j, fi, src, exp, rows: (exp[j], 0, fchunk(j, fi, rows))),
                pl.BlockSpec((1, tf, d), lambda j, fi, src, exp, rows: (exp[j], fchunk(j, fi, rows), 0)),
            ],
            out_specs=pl.BlockSpec((tm, d), lambda j, fi, src, exp, rows: (j, 0)),
            scratch_shapes=[pltpu.VMEM((tm, d), BF16), pltpu.VMEM((tm, d), F32)],
        ),
        compiler_params=_cparams(("arbitrary", "arbitrary")),
        name="moe_experts",
    )(tile_src, tile_expert, tile_rows, xs, wg, wu, wd)


def _combine_kernel(pos_ref, posn_ref, x_ref, gate_ref, g2_ref, fw_ref, ys_ref, out_ref, ybuf, sems, *, rows):
    i = pl.program_id(0)
    slot = i % 2

    def issue(p_ref, sl):
        def body(r, carry):
            for k in range(TOP_K):
                pltpu.make_async_copy(ys_ref.at[pl.ds(p_ref[0, k, r], 1)], ybuf.at[sl, pl.ds(k * rows + r, 1)],
                                      sems.at[sl]).start()
            return carry
        lax.fori_loop(0, rows, body, 0, unroll=DMA_ISSUE_UNROLL)

    @pl.when(i == 0)
    def _():
        issue(pos_ref, 0)

    @pl.when(i + 1 < pl.num_programs(0))
    def _():
        issue(posn_ref, 1 - slot)

    pltpu.make_async_copy(ys_ref.at[pl.ds(0, TOP_K * rows)], ybuf.at[slot], sems.at[slot]).wait()
    g = gate_ref[...]
    y = g[:, 0:1] * ybuf[slot, 0:rows] + g[:, 1:2] * ybuf[slot, rows:2 * rows]
    out_ref[...] = _rms(x_ref[...] + g2_ref[0] * y, fw_ref[...])


def _combine(x, ys, pos, gates, g2, fw, rows):
    n, d = x.shape
    n_steps = n // rows
    steps_per_batch = n_steps // g2.shape[0]
    assert n % rows == 0 and n_steps % g2.shape[0] == 0
    pos3 = jnp.swapaxes(pos.reshape(TOP_K, n_steps, rows), 0, 1)
    smem_tile = functools.partial(pl.BlockSpec, (1, TOP_K, rows), memory_space=pltpu.SMEM)
    return pl.pallas_call(
        functools.partial(_combine_kernel, rows=rows),
        out_shape=jax.ShapeDtypeStruct((n, d), F32),
        grid=(n_steps,),
        in_specs=[
            smem_tile(lambda i: (i, 0, 0)),
            smem_tile(lambda i: (jnp.minimum(i + 1, n_steps - 1), 0, 0)),
            pl.BlockSpec((rows, d), lambda i: (i, 0)),
            pl.BlockSpec((rows, TOP_K), lambda i: (i, 0)),
            pl.BlockSpec((1, 1, d), lambda i: (i // steps_per_batch, 0, 0)),
            pl.BlockSpec((1, d), lambda i: (0, 0)),
            pl.BlockSpec(memory_space=pl.ANY),
        ],
        out_specs=pl.BlockSpec((rows, d), lambda i: (i, 0)),
        scratch_shapes=[pltpu.VMEM((2, TOP_K * rows, d), F32), pltpu.SemaphoreType.DMA((2,))],
        compiler_params=_cparams(("arbitrary",)),
        name="moe_combine",
    )(pos3, pos3, x, gates, g2, fw, ys)


def _pick(n, prefs):
    for p in prefs:
        if n % p == 0:
            return p
    return n


def kernel(x, c, ctx, c_ctx, w_mod, b_mod, norm1_w, norm2_w, final_norm_w, da_w_qkv, da_w_o, da_lambda_q1, da_lambda_k1, da_lambda_q2, da_lambda_k2, da_subln_w, gm_w_in, gm_vnorm_w, gm_vnorm_b, gm_w_s, gm_b_s, gm_w_out, ffn_w_gate, ffn_w_up, ffn_w_down, moe_w_router, moe_w_gate, moe_w_up, moe_w_down):
    b, s_len, d = x.shape
    ctx_len = ctx.shape[1]

    pad = (-(b + 1)) % 8
    cond = jnp.concatenate([c, c_ctx[None, :], jnp.zeros((pad, d), F32)], axis=0)
    mod = _ada(cond, w_mod, b_mod)

    def chunks(layer, r0, r1):
        m = mod[layer, r0:r1]
        return [m[:, None, i * d:(i + 1) * d] for i in range(6)]

    sh1, sc1, g1, sh2, sc2, g2 = chunks(0, 0, b)
    csh1, csc1 = chunks(0, b, b + 1)[:2]
    sh1b, sc1b, g1b, sh2b, sc2b, g2b = chunks(1, 0, b)

    ts = _pick(ctx_len, (256, 128))
    wq = da_w_qkv[0, :, 0:QK_WIDTH]
    wk = da_w_qkv[0, :, QK_WIDTH:2 * QK_WIDTH]
    wv = da_w_qkv[0, :, 2 * QK_WIDTH:]
    k_all, qt, vt5 = _qkv(x, ctx, norm1_w[0][None], sh1, sc1, csh1, csc1,
                          wq.T.astype(BF16), wk.astype(BF16), wv.T.astype(BF16), ts)
    lam_init = 0.8 - 0.6 * math.exp(-0.3 * 0)
    o = _attention(qt, k_all, vt5, da_lambda_q1[0][None], da_lambda_k1[0][None], da_lambda_q2[0][None],
                   da_lambda_k2[0][None], da_subln_w[0][:, None], lam_init, _pick(s_len, (512, 256, 128)))
    x2 = _ffn(x, o, da_w_o[0].astype(BF16), g1, norm2_w[0][None], sh2, sc2, g2,
              ffn_w_gate[0].astype(BF16), ffn_w_up[0].astype(BF16), ffn_w_down[0].astype(BF16),
              _pick(s_len, (256, 128)))

    x3, h3, idx, gates = _gmlp(x2, norm1_w[1][None], sh1b, sc1b, g1b, gm_w_in[0].astype(BF16),
                               gm_vnorm_w[0][None], gm_vnorm_b[0][None], gm_w_s[0].astype(BF16),
                               gm_b_s[0][:, :, None], gm_w_out[0].astype(BF16),
                               norm2_w[1][None], sh2b, sc2b, moe_w_router[0].T, _pick(s_len, (256, 128)))
    n = b * s_len
    ne, _, f = moe_w_gate[0].shape
    tm = _pick(TOP_K * n, (1024, 512, 256, 128))
    pos, tile_src, tile_expert, tile_rows = _route(idx, ne, tm)
    xs = _dispatch(h3.reshape(n, d), pos, tile_src.shape[0] * tm, _pick(n, (512, 256, 128)))
    ys = _experts(xs, tile_src, tile_expert, tile_rows, moe_w_gate[0], moe_w_up[0], moe_w_down[0],
                  tm, _pick(f, (512, 256, 128)))
    gates_t = jnp.swapaxes(gates, 1, 2).reshape(n, TOP_K)
    out = _combine(x3.reshape(n, d), ys, pos, gates_t, g2b, final_norm_w[None], _pick(s_len, (256, 128)))
    return out.reshape(b, s_len, d)
```

```python
import functools
import math

import jax
import jax.numpy as jnp
from jax import lax
from jax.experimental import pallas as pl
from jax.experimental.pallas import tpu as pltpu

F32 = jnp.float32
BF16 = jnp.bfloat16

EPS = 1e-6
GRID_W = 64
ROPE_BASE = 10000.0
N_HEADS = 8
HEAD_DIM = 64
ROPE_FREQS = HEAD_DIM // 4
V_DIM = 2 * HEAD_DIM
QK_WIDTH = N_HEADS * 2 * HEAD_DIM
V_WIDTH = N_HEADS * V_DIM
V_ROWS = V_DIM + 16
KEY_TILES_PER_BLOCK = (5, 4, 3, 2)
SCORE_BUFFERS = 2
KEY_BLOCK_SPLIT = 1
STEPS_PER_TRIP = 1
DMA_ISSUE_UNROLL = 4
GM_CHUNK = 128
GM_GROUPS = 8
TOP_K = 2
NEG_BIG = -1e30
LOG2_E = math.log2(math.e)

LANES = 128
VMEM_LIMIT_BYTES = 56 * 1024 * 1024


def _cparams(semantics):
    return pltpu.CompilerParams(dimension_semantics=semantics, vmem_limit_bytes=VMEM_LIMIT_BYTES)


def _rms(x, w):
    return x * lax.rsqrt(jnp.mean(x * x, axis=-1, keepdims=True) + EPS) * w


def _silu(x):
    return x * jax.nn.sigmoid(x)


def _const_spec(shape):
    nd = len(shape)
    return pl.BlockSpec(shape, lambda *_: (0,) * nd, pipeline_mode=pl.Buffered(1))


def _ada_kernel(cond_ref, w_ref, b_ref, o_ref):
    s = _silu(cond_ref[...])
    o_ref[0] = jnp.dot(s, w_ref[0], preferred_element_type=F32, precision=lax.Precision.HIGHEST) + b_ref[0]


def _ada(cond, w_mod, b_mod):
    depth, d, n = w_mod.shape
    rows = cond.shape[0]
    tn = n // 4
    return pl.pallas_call(
        _ada_kernel,
        out_shape=jax.ShapeDtypeStruct((depth, rows, n), F32),
        grid=(depth, n // tn),
        in_specs=[
            pl.BlockSpec((rows, d), lambda l, j: (0, 0)),
            pl.BlockSpec((1, d, tn), lambda l, j: (l, 0, j)),
            pl.BlockSpec((1, 1, tn), lambda l, j: (l, 0, j)),
        ],
        out_specs=pl.BlockSpec((1, rows, tn), lambda l, j: (l, 0, j)),
        compiler_params=_cparams(("arbitrary", "arbitrary")),
        name="ada",
    )(cond, w_mod, b_mod.reshape(depth, 1, n))


def _qkv_kernel(x_ref, ctx_ref, nw_ref, sh_ref, sc_ref, csh_ref, csc_ref, wqt_ref, wk_ref, wvt_ref,
                ck_ref, sk_ref, cq_ref, sq_ref, k_ref, qt_ref, vt_ref, *, n_ctx, ts):
    is_ctx = pl.program_id(1) < n_ctx
    xin = jnp.where(is_ctx, ctx_ref[0], x_ref[0])
    sh = jnp.where(is_ctx, csh_ref[0], sh_ref[0])
    sc = jnp.where(is_ctx, csc_ref[0], sc_ref[0])
    h = (_rms(xin, nw_ref[...]) * (1.0 + sc) + sh).astype(BF16)

    k = jnp.dot(h, wk_ref[...], preferred_element_type=F32)
    lane = lax.broadcasted_iota(jnp.int32, (ts, LANES), 1)
    lower = (lane % 32) < 16
    ck = ck_ref[...]
    sk = sk_ref[...]
    for j in range(QK_WIDTH // LANES):
        kb = k[:, j * LANES:(j + 1) * LANES]
        partner = jnp.where(lower, pltpu.roll(kb, LANES - 16, 1), pltpu.roll(kb, 16, 1))
        k_ref[0, :, j * LANES:(j + 1) * LANES] = (kb * ck + partner * sk).astype(BF16)

    nt = (((1,), (1,)), ((), ()))
    qt = lax.dot_general(wqt_ref[...], h, nt, preferred_element_type=F32)
    cq = cq_ref[...]
    sq = sq_ref[...]
    scale = HEAD_DIM ** -0.5 * LOG2_E
    for g in range(QK_WIDTH // HEAD_DIM):
        blk = qt[g * HEAD_DIM:(g + 1) * HEAD_DIM]
        partner = jnp.concatenate([blk[16:32], blk[0:16], blk[48:64], blk[32:48]], axis=0)
        qt_ref[0, g * HEAD_DIM:(g + 1) * HEAD_DIM, :] = ((blk * cq + partner * sq) * scale).astype(BF16)

    vt = lax.dot_general(wvt_ref[...], h, nt, preferred_element_type=F32)
    ones_row = (lax.broadcasted_iota(jnp.int32, (V_ROWS - V_DIM, ts), 0) == 0).astype(BF16)
    for hh in range(N_HEADS):
        vt_ref[0, hh, 0, 0:V_DIM, :] = vt[hh * V_DIM:(hh + 1) * V_DIM].astype(BF16)
        vt_ref[0, hh, 0, V_DIM:V_ROWS, :] = ones_row


def _rope_tables(s_len, ctx_len):
    pos = jnp.arange(s_len)
    row = (pos // GRID_W).astype(F32)
    col = (pos % GRID_W).astype(F32)
    inv_freq = ROPE_BASE ** (-jnp.arange(ROPE_FREQS, dtype=F32) / ROPE_FREQS)
    ar = row[:, None] * inv_freq
    ac = col[:, None] * inv_freq
    cos64 = jnp.concatenate([jnp.cos(ar), jnp.cos(ar), jnp.cos(ac), jnp.cos(ac)], axis=-1)
    sin64 = jnp.concatenate([-jnp.sin(ar), jnp.sin(ar), -jnp.sin(ac), jnp.sin(ac)], axis=-1)
    ck = jnp.concatenate([jnp.ones((ctx_len, LANES), F32), jnp.tile(cos64, (1, 2))], axis=0)
    sk = jnp.concatenate([jnp.zeros((ctx_len, LANES), F32), jnp.tile(sin64, (1, 2))], axis=0)
    return ck, sk, cos64.T, sin64.T


def _qkv(x, ctx, nw, sh, sc, csh, csc, wqt, wk, wvt, ts):
    b, s_len, d = x.shape
    ctx_len = ctx.shape[1]
    assert s_len % ts == 0 and ctx_len % ts == 0
    n_ctx = ctx_len // ts
    n_lat = s_len // ts
    ktot = ctx_len + s_len
    sub = _pick(n_ctx + n_lat, KEY_TILES_PER_BLOCK)
    ck, sk, cq, sq = _rope_tables(s_len, ctx_len)

    def lat(t):
        return jnp.maximum(t - n_ctx, 0)

    row = lambda bb, t: (bb, 0, 0)
    return pl.pallas_call(
        functools.partial(_qkv_kernel, n_ctx=n_ctx, ts=ts),
        out_shape=(
            jax.ShapeDtypeStruct((b, ktot, QK_WIDTH), BF16),
            jax.ShapeDtypeStruct((b, QK_WIDTH, s_len), BF16),
            jax.ShapeDtypeStruct((b, N_HEADS, ktot // (ts * sub), V_ROWS, ts * sub), BF16),
        ),
        grid=(b, n_ctx + n_lat),
        in_specs=[
            pl.BlockSpec((1, ts, d), lambda bb, t: (bb, lat(t), 0)),
            pl.BlockSpec((1, ts, d), lambda bb, t: (bb, jnp.minimum(t, n_ctx - 1), 0)),
            _const_spec((1, d)),
            pl.BlockSpec((1, 1, d), row),
            pl.BlockSpec((1, 1, d), row),
            _const_spec((1, 1, d)),
            _const_spec((1, 1, d)),
            _const_spec((QK_WIDTH, d)),
            _const_spec((d, QK_WIDTH)),
            _const_spec((V_WIDTH, d)),
            pl.BlockSpec((ts, LANES), lambda bb, t: (t, 0)),
            pl.BlockSpec((ts, LANES), lambda bb, t: (t, 0)),
            pl.BlockSpec((HEAD_DIM, ts), lambda bb, t: (0, lat(t))),
            pl.BlockSpec((HEAD_DIM, ts), lambda bb, t: (0, lat(t))),
        ],
        out_specs=(
            pl.BlockSpec((1, ts, QK_WIDTH), lambda bb, t: (bb, t, 0)),
            pl.BlockSpec((1, QK_WIDTH, ts), lambda bb, t: (bb, 0, lat(t))),
            pl.BlockSpec((1, N_HEADS, 1, V_ROWS, ts), lambda bb, t: (bb, 0, t // sub, 0, t % sub)),
        ),
        compiler_params=_cparams(("arbitrary", "arbitrary")),
        name="qkv",
    )(x, ctx, nw, sh, sc, csh, csc, wqt, wk, wvt, ck, sk, cq, sq)


def _attn_kernel(lq1_ref, lk1_ref, lq2_ref, lk2_ref, subw_ref, q_ref, k_ref, v_ref, o_ref, w_scr, acc_scr,
                 *bufs, tq, tkb, nb, split, lam_init):
    nbuf = len(bufs) // 2
    assert nbuf % split == 0
    tks = tkb // split
    nb = nb * split
    s_bufs, p_bufs = bufs[:nbuf], bufs[nbuf:]
    q = q_ref[0]
    zero = jnp.zeros((HEAD_DIM, tq), BF16)
    w_scr[0:HEAD_DIM, 0:tq] = q[0:HEAD_DIM]
    w_scr[0:HEAD_DIM, tq:2 * tq] = zero
    w_scr[HEAD_DIM:2 * HEAD_DIM, 0:tq] = zero
    w_scr[HEAD_DIM:2 * HEAD_DIM, tq:2 * tq] = q[HEAD_DIM:2 * HEAD_DIM]
    acc_scr[...] = jnp.zeros_like(acc_scr)

    def scores(blk):
        off = pl.multiple_of(blk * tks, tks)
        return jnp.dot(k_ref[0, pl.ds(off, tks), :], w_scr[...], preferred_element_type=F32)

    def softmax(slot, m):
        s = s_bufs[slot][...]
        m_new = jnp.maximum(m, jnp.max(s, axis=0, keepdims=True))
        p_bufs[slot][...] = jnp.exp2(s - m_new).astype(BF16)
        return m_new, jnp.exp2(m - m_new)

    def values(j, slot, alpha):
        part = slot % split
        vt = v_ref[0, 0, j // split, :, part * tks:(part + 1) * tks]
        pv = jnp.dot(vt, p_bufs[slot][...], preferred_element_type=F32)
        acc_scr[...] = acc_scr[...] * alpha + pv

    def step(j, slot, m, alpha):
        s_bufs[(slot + 2) % nbuf][...] = scores(j + 2)
        m, alpha_next = softmax((slot + 1) % nbuf, m)
        values(j, slot, alpha)
        return m, alpha_next

    s_bufs[0][...] = scores(0)
    m, alpha = softmax(0, jnp.full((1, 2 * tq), NEG_BIG, F32))
    if nb > 1:
        s_bufs[1][...] = scores(1)
    n_full = max(nb - 2, 0)

    per_trip = nbuf * STEPS_PER_TRIP

    def unrolled(i, carry):
        for r in range(per_trip):
            carry = step(per_trip * i + r, r % nbuf, *carry)
        return carry

    m, alpha = lax.fori_loop(0, n_full // per_trip, unrolled, (m, alpha))
    for j in range(per_trip * (n_full // per_trip), n_full):
        m, alpha = step(j, j % nbuf, m, alpha)
    if nb > 1:
        m, alpha_next = softmax((nb - 1) % nbuf, m)
        values(nb - 2, (nb - 2) % nbuf, alpha)
        alpha = alpha_next
    values(nb - 1, (nb - 1) % nbuf, alpha)

    lam = (jnp.exp(jnp.sum(lq1_ref[...] * lk1_ref[...], axis=-1, keepdims=True))
           - jnp.exp(jnp.sum(lq2_ref[...] * lk2_ref[...], axis=-1, keepdims=True)) + lam_init)
    acc = acc_scr[0:V_DIM, :]
    inv = 1.0 / acc_scr[V_DIM:V_DIM + 1, :]
    o = acc[:, 0:tq] * inv[:, 0:tq] - lam * (acc[:, tq:2 * tq] * inv[:, tq:2 * tq])
    y = o * lax.rsqrt(jnp.mean(o * o, axis=0, keepdims=True) + EPS) * subw_ref[...]
    y = y * (1.0 - lam_init)
    o_ref[0] = y.T.astype(BF16)


def _attention(qt, k_all, vt5, lq1, lk1, lq2, lk2, subw, lam_init, tq):
    b, _, s_len = qt.shape
    ktot = k_all.shape[1]
    nb, tkb = vt5.shape[2], vt5.shape[4]
    assert s_len % tq == 0 and nb * tkb == ktot and vt5.shape[3] == V_ROWS
    split = KEY_BLOCK_SPLIT if tkb % (KEY_BLOCK_SPLIT * LANES) == 0 else 1
    tks = tkb // split
    vec = _const_spec((1, HEAD_DIM))
    return pl.pallas_call(
        functools.partial(_attn_kernel, tq=tq, tkb=tkb, nb=nb, split=split, lam_init=lam_init),
        out_shape=jax.ShapeDtypeStruct((b, s_len, V_WIDTH), BF16),
        grid=(b, N_HEADS, s_len // tq),
        in_specs=[
            vec, vec, vec, vec,
            _const_spec((V_DIM, 1)),
            pl.BlockSpec((1, 2 * HEAD_DIM, tq), lambda bb, hh, i: (bb, hh, i)),
            pl.BlockSpec((1, ktot, 2 * HEAD_DIM), lambda bb, hh, i: (bb, 0, hh)),
            pl.BlockSpec((1, 1, nb, V_ROWS, tkb), lambda bb, hh, i: (bb, hh, 0, 0, 0)),
        ],
        out_specs=pl.BlockSpec((1, tq, V_DIM), lambda bb, hh, i: (bb, i, hh)),
        scratch_shapes=[
            pltpu.VMEM((2 * HEAD_DIM, 2 * tq), BF16),
            pltpu.VMEM((V_ROWS, 2 * tq), F32),
        ] + [pltpu.VMEM((tks, 2 * tq), F32)] * SCORE_BUFFERS + [pltpu.VMEM((tks, 2 * tq), BF16)] * SCORE_BUFFERS,
        compiler_params=_cparams(("arbitrary", "arbitrary", "arbitrary")),
        name="attention",
    )(lq1, lk1, lq2, lk2, subw, qt, k_all, vt5)


def _ffn_kernel(x_ref, o_ref, wo_ref, g1_ref, nw_ref, sh_ref, sc_ref, g2_ref, wg_ref, wu_ref, wd_ref, out_ref):
    y = jnp.dot(o_ref[0], wo_ref[...], preferred_element_type=F32)
    x1 = x_ref[0] + g1_ref[0] * y
    h = (_rms(x1, nw_ref[...]) * (1.0 + sc_ref[0]) + sh_ref[0]).astype(BF16)
    a = jnp.dot(h, wg_ref[...], preferred_element_type=F32)
    u = jnp.dot(h, wu_ref[...], preferred_element_type=F32)
    z = (_silu(a) * u).astype(BF16)
    out_ref[0] = x1 + g2_ref[0] * jnp.dot(z, wd_ref[...], preferred_element_type=F32)


def _ffn(x, o, wo, g1, nw, sh, sc, g2, wg, wu, wd, ts):
    b, s_len, d = x.shape
    f = wg.shape[1]
    row = lambda bb, t: (bb, 0, 0)
    tile = lambda bb, t: (bb, t, 0)
    vec = pl.BlockSpec((1, 1, d), row)
    return pl.pallas_call(
        _ffn_kernel,
        out_shape=jax.ShapeDtypeStruct((b, s_len, d), F32),
        grid=(b, s_len // ts),
        in_specs=[
            pl.BlockSpec((1, ts, d), tile),
            pl.BlockSpec((1, ts, V_WIDTH), tile),
            _const_spec((V_WIDTH, d)),
            vec, _const_spec((1, d)), vec, vec, vec,
            _const_spec((d, f)), _const_spec((d, f)), _const_spec((f, d)),
        ],
        out_specs=pl.BlockSpec((1, ts, d), tile),
        compiler_params=_cparams(("arbitrary", "arbitrary")),
        name="ffn",
    )(x, o, wo, g1, nw, sh, sc, g2, wg, wu, wd)


def _gmlp_kernel(x_ref, nw1_ref, sh1_ref, sc1_ref, g1_ref, win_ref, vnw_ref, vnb_ref, ws_ref, bs_ref, wout_ref,
                 nw2_ref, sh2_ref, sc2_ref, wrt_ref, x3_ref, h3_ref, idx_ref, gate_ref, t_scr, *, ts, half):
    x2 = x_ref[0]
    h = (_rms(x2, nw1_ref[...]) * (1.0 + sc1_ref[0]) + sh1_ref[0]).astype(BF16)
    z = jnp.dot(h, win_ref[...], preferred_element_type=F32)
    z = 0.5 * z * (1.0 + lax.erf(z * (2.0 ** -0.5)))
    u = z[:, 0:half]
    v = z[:, half:2 * half]
    mu = jnp.mean(v, axis=-1, keepdims=True)
    vc = v - mu
    v = vc * lax.rsqrt(jnp.mean(vc * vc, axis=-1, keepdims=True) + EPS) * vnw_ref[...] + vnb_ref[...]
    vb = v.astype(BF16)
    gd = half // GM_GROUPS
    for n in range(ts // GM_CHUNK):
        rows = slice(n * GM_CHUNK, (n + 1) * GM_CHUNK)
        for g in range(GM_GROUPS):
            cols = slice(g * gd, (g + 1) * gd)
            s = jnp.dot(ws_ref[g], vb[rows, cols], preferred_element_type=F32) + bs_ref[g]
            t_scr[rows, cols] = (u[rows, cols] * s).astype(BF16)
    x3 = x2 + g1_ref[0] * jnp.dot(t_scr[...], wout_ref[...], preferred_element_type=F32)
    x3_ref[0] = x3

    h3 = _rms(x3, nw2_ref[...]) * (1.0 + sc2_ref[0]) + sh2_ref[0]
    h3_ref[0] = h3
    nt = (((1,), (1,)), ((), ()))
    logits = lax.dot_general(wrt_ref[...], h3, nt, preferred_element_type=F32,
                             precision=lax.Precision.HIGHEST)
    ne = logits.shape[0]
    eid = lax.broadcasted_iota(jnp.int32, logits.shape, 0)
    m1 = jnp.max(logits, axis=0, keepdims=True)
    i1 = jnp.min(jnp.where(logits == m1, eid, ne), axis=0, keepdims=True)
    rest = jnp.where(eid == i1, -jnp.inf, logits)
    m2 = jnp.max(rest, axis=0, keepdims=True)
    i2 = jnp.min(jnp.where(rest == m2, eid, ne), axis=0, keepdims=True)
    e2 = jnp.exp(m2 - m1)
    den = 1.0 + e2
    idx_ref[0] = jnp.concatenate([i1, i2], axis=0)
    gate_ref[0] = jnp.concatenate([1.0 / den, e2 / den], axis=0)


def _gmlp(x, nw1, sh1, sc1, g1, win, vnw, vnb, ws, bs, wout, nw2, sh2, sc2, wrt, ts):
    b, s_len, d = x.shape
    half = wout.shape[0]
    ne = wrt.shape[0]
    assert ts % GM_CHUNK == 0 and s_len % ts == 0
    row = lambda bb, t: (bb, 0, 0)
    tile = lambda bb, t: (bb, t, 0)
    vec = pl.BlockSpec((1, 1, d), row)
    return pl.pallas_call(
        functools.partial(_gmlp_kernel, ts=ts, half=half),
        out_shape=(
            jax.ShapeDtypeStruct((b, s_len, d), F32),
            jax.ShapeDtypeStruct((b, s_len, d), F32),
            jax.ShapeDtypeStruct((b, TOP_K, s_len), jnp.int32),
            jax.ShapeDtypeStruct((b, TOP_K, s_len), F32),
        ),
        grid=(b, s_len // ts),
        in_specs=[
            pl.BlockSpec((1, ts, d), tile),
            _const_spec((1, d)), vec, vec, vec,
            _const_spec((d, 2 * half)),
            _const_spec((1, half)), _const_spec((1, half)),
            _const_spec((GM_GROUPS, GM_CHUNK, GM_CHUNK)),
            _const_spec((GM_GROUPS, GM_CHUNK, 1)),
            _const_spec((half, d)),
            _const_spec((1, d)), vec, vec,
            _const_spec((ne, d)),
        ],
        out_specs=(
            pl.BlockSpec((1, ts, d), tile),
            pl.BlockSpec((1, ts, d), tile),
            pl.BlockSpec((1, TOP_K, ts), lambda bb, t: (bb, 0, t)),
            pl.BlockSpec((1, TOP_K, ts), lambda bb, t: (bb, 0, t)),
        ),
        scratch_shapes=[pltpu.VMEM((ts, half), BF16)],
        compiler_params=_cparams(("arbitrary", "arbitrary")),
        name="gmlp",
    )(x, nw1, sh1, sc1, g1, win, vnw, vnb, ws, bs, wout, nw2, sh2, sc2, wrt)


def _route(idx, n_experts, tm):
    b, k, s_len = idx.shape
    n = b * s_len
    e_flat = jnp.swapaxes(idx, 0, 1).reshape(k * n)
    onehot = (e_flat[:, None] == jnp.arange(n_experts, dtype=jnp.int32)[None, :]).astype(jnp.int32)
    csum = jnp.cumsum(onehot, axis=0)
    counts = csum[-1]
    rank = jnp.take_along_axis(csum, e_flat[:, None], axis=1)[:, 0] - 1
    padded = ((counts + tm - 1) // tm) * tm
    ends = jnp.cumsum(padded)
    starts = ends - padded
    pos = (starts[e_flat] + rank).astype(jnp.int32)
    n_tiles = (k * n) // tm + n_experts
    tile_ids = jnp.arange(n_tiles, dtype=jnp.int32)
    used = tile_ids * tm < ends[-1]
    tile_src = jnp.where(used, tile_ids, ends[-1] // tm - 1).astype(jnp.int32)
    tile_expert = jnp.minimum(jnp.searchsorted(ends, tile_src * tm, side="right"), n_experts - 1).astype(jnp.int32)
    tokens = jnp.tile(jnp.arange(n, dtype=jnp.int32), k)
    slot_token = jnp.zeros((n_tiles * tm,), jnp.int32).at[pos].set(tokens, unique_indices=True)
    return pos.reshape(k, n), slot_token, tile_expert, used.astype(jnp.int32)


def _experts_kernel(exp_ref, used_ref, tok_ref, tokn_ref, h_ref, wg_ref, wu_ref, wd_ref, y_ref,
                    xbuf, xb_scr, acc_scr, sems, *, tm):
    j = pl.program_id(0)
    f = pl.program_id(1)
    n_tiles = pl.num_programs(0)
    used = used_ref[j] > 0
    slot = j % 2

    def gather(t_ref, sl):
        def body(r, carry):
            pltpu.make_async_copy(h_ref.at[pl.ds(t_ref[0, 0, r], 1)], xbuf.at[sl, pl.ds(r, 1)], sems.at[sl]).start()
            return carry
        lax.fori_loop(0, tm, body, 0, unroll=DMA_ISSUE_UNROLL)

    @pl.when(f == 0)
    def _():
        @pl.when(j == 0)
        def _():
            gather(tok_ref, 0)

        @pl.when((j + 1 < n_tiles) & (used_ref[jnp.minimum(j + 1, n_tiles - 1)] > 0))
        def _():
            gather(tokn_ref, 1 - slot)

        @pl.when(used)
        def _():
            pltpu.make_async_copy(h_ref.at[pl.ds(0, tm)], xbuf.at[slot], sems.at[slot]).wait()
            xb_scr[...] = xbuf[slot].astype(BF16)

        acc_scr[...] = jnp.zeros_like(acc_scr)

    @pl.when(used)
    def _():
        xb = xb_scr[...]
        a = jnp.dot(xb, wg_ref[0].astype(BF16), preferred_element_type=F32)
        u = jnp.dot(xb, wu_ref[0].astype(BF16), preferred_element_type=F32)
        z = (_silu(a) * u).astype(BF16)
        acc_scr[...] += jnp.dot(z, wd_ref[0].astype(BF16), preferred_element_type=F32)

    @pl.when(f == pl.num_programs(1) - 1)
    def _():
        y_ref[...] = acc_scr[...]


def _experts(h, slot_token, tile_expert, tile_used, wg, wu, wd, tm, tf):
    n, d = h.shape
    ne, _, f = wg.shape
    nf = f // tf
    n_tiles = tile_expert.shape[0]
    assert slot_token.shape[0] == n_tiles * tm and f % tf == 0
    tok3 = slot_token.reshape(n_tiles, 1, tm)

    def fchunk(j, fi, used):
        return jnp.where(used[j] > 0, fi, nf - 1)

    smem_tile = functools.partial(pl.BlockSpec, (1, 1, tm), memory_space=pltpu.SMEM)
    return pl.pallas_call(
        functools.partial(_experts_kernel, tm=tm),
        out_shape=jax.ShapeDtypeStruct((n_tiles * tm, d), F32),
        grid_spec=pltpu.PrefetchScalarGridSpec(
            num_scalar_prefetch=2,
            grid=(n_tiles, nf),
            in_specs=[
                smem_tile(lambda j, fi, exp, used: (j, 0, 0)),
                smem_tile(lambda j, fi, exp, used: (jnp.minimum(j + 1, n_tiles - 1), 0, 0)),
                pl.BlockSpec(memory_space=pl.ANY),
                pl.BlockSpec((1, d, tf), lambda j, fi, exp, used: (exp[j], 0, fchunk(j, fi, used))),
                pl.BlockSpec((1, d, tf), lambda j, fi, exp, used: (exp[j], 0, fchunk(j, fi, used))),
                pl.BlockSpec((1, tf, d), lambda j, fi, exp, used: (exp[j], fchunk(j, fi, used), 0)),
            ],
            out_specs=pl.BlockSpec((tm, d), lambda j, fi, exp, used: (j, 0)),
            scratch_shapes=[pltpu.VMEM((2, tm, d), F32), pltpu.VMEM((tm, d), BF16), pltpu.VMEM((tm, d), F32),
                            pltpu.SemaphoreType.DMA((2,))],
        ),
        compiler_params=_cparams(("arbitrary", "arbitrary")),
        name="moe_experts",
    )(tile_expert, tile_used, tok3, tok3, h, wg, wu, wd)


def _combine_kernel(pos_ref, posn_ref, x_ref, gate_ref, g2_ref, fw_ref, ys_ref, out_ref, ybuf, sems, *, rows):
    i = pl.program_id(0)
    slot = i % 2

    def issue(p_ref, sl):
        def body(r, carry):
            for k in range(TOP_K):
                pltpu.make_async_copy(ys_ref.at[pl.ds(p_ref[0, k, r], 1)], ybuf.at[sl, pl.ds(k * rows + r, 1)],
                                      sems.at[sl]).start()
            return carry
        lax.fori_loop(0, rows, body, 0, unroll=DMA_ISSUE_UNROLL)

    @pl.when(i == 0)
    def _():
        issue(pos_ref, 0)

    @pl.when(i + 1 < pl.num_programs(0))
    def _():
        issue(posn_ref, 1 - slot)

    pltpu.make_async_copy(ys_ref.at[pl.ds(0, TOP_K * rows)], ybuf.at[slot], sems.at[slot]).wait()
    g = gate_ref[...]
    y = g[:, 0:1] * ybuf[slot, 0:rows] + g[:, 1:2] * ybuf[slot, rows:2 * rows]
    out_ref[...] = _rms(x_ref[...] + g2_ref[0] * y, fw_ref[...])


def _combine(x, ys, pos, gates, g2, fw, rows):
    n, d = x.shape
    n_steps = n // rows
    steps_per_batch = n_steps // g2.shape[0]
    assert n % rows == 0 and n_steps % g2.shape[0] == 0
    pos3 = jnp.swapaxes(pos.reshape(TOP_K, n_steps, rows), 0, 1)
    smem_tile = functools.partial(pl.BlockSpec, (1, TOP_K, rows), memory_space=pltpu.SMEM)
    return pl.pallas_call(
        functools.partial(_combine_kernel, rows=rows),
        out_shape=jax.ShapeDtypeStruct((n, d), F32),
        grid=(n_steps,),
        in_specs=[
            smem_tile(lambda i: (i, 0, 0)),
            smem_tile(lambda i: (jnp.minimum(i + 1, n_steps - 1), 0, 0)),
            pl.BlockSpec((rows, d), lambda i: (i, 0)),
            pl.BlockSpec((rows, TOP_K), lambda i: (i, 0)),
            pl.BlockSpec((1, 1, d), lambda i: (i // steps_per_batch, 0, 0)),
            pl.BlockSpec((1, d), lambda i: (0, 0)),
            pl.BlockSpec(memory_space=pl.ANY),
        ],
        out_specs=pl.BlockSpec((rows, d), lambda i: (i, 0)),
        scratch_shapes=[pltpu.VMEM((2, TOP_K * rows, d), F32), pltpu.SemaphoreType.DMA((2,))],
        compiler_params=_cparams(("arbitrary",)),
        name="moe_combine",
    )(pos3, pos3, x, gates, g2, fw, ys)


def _pick(n, prefs):
    for p in prefs:
        if n % p == 0:
            return p
    return n


def kernel(x, c, ctx, c_ctx, w_mod, b_mod, norm1_w, norm2_w, final_norm_w, da_w_qkv, da_w_o, da_lambda_q1, da_lambda_k1, da_lambda_q2, da_lambda_k2, da_subln_w, gm_w_in, gm_vnorm_w, gm_vnorm_b, gm_w_s, gm_b_s, gm_w_out, ffn_w_gate, ffn_w_up, ffn_w_down, moe_w_router, moe_w_gate, moe_w_up, moe_w_down):
    b, s_len, d = x.shape
    ctx_len = ctx.shape[1]

    pad = (-(b + 1)) % 8
    cond = jnp.concatenate([c, c_ctx[None, :], jnp.zeros((pad, d), F32)], axis=0)
    mod = _ada(cond, w_mod, b_mod)

    def chunks(layer, r0, r1):
        m = mod[layer, r0:r1]
        return [m[:, None, i * d:(i + 1) * d] for i in range(6)]

    sh1, sc1, g1, sh2, sc2, g2 = chunks(0, 0, b)
    csh1, csc1 = chunks(0, b, b + 1)[:2]
    sh1b, sc1b, g1b, sh2b, sc2b, g2b = chunks(1, 0, b)

    ts = _pick(ctx_len, (256, 128))
    wq = da_w_qkv[0, :, 0:QK_WIDTH]
    wk = da_w_qkv[0, :, QK_WIDTH:2 * QK_WIDTH]
    wv = da_w_qkv[0, :, 2 * QK_WIDTH:]
    k_all, qt, vt5 = _qkv(x, ctx, norm1_w[0][None], sh1, sc1, csh1, csc1,
                          wq.T.astype(BF16), wk.astype(BF16), wv.T.astype(BF16), ts)
    lam_init = 0.8 - 0.6 * math.exp(-0.3 * 0)
    o = _attention(qt, k_all, vt5, da_lambda_q1[0][None], da_lambda_k1[0][None], da_lambda_q2[0][None],
                   da_lambda_k2[0][None], da_subln_w[0][:, None], lam_init, _pick(s_len, (512, 256, 128)))
    x2 = _ffn(x, o, da_w_o[0].astype(BF16), g1, norm2_w[0][None], sh2, sc2, g2,
              ffn_w_gate[0].astype(BF16), ffn_w_up[0].astype(BF16), ffn_w_down[0].astype(BF16),
              _pick(s_len, (256, 128)))

    x3, h3, idx, gates = _gmlp(x2, norm1_w[1][None], sh1b, sc1b, g1b, gm_w_in[0].astype(BF16),
                               gm_vnorm_w[0][None], gm_vnorm_b[0][None], gm_w_s[0].astype(BF16),
                               gm_b_s[0][:, :, None], gm_w_out[0].astype(BF16),
                               norm2_w[1][None], sh2b, sc2b, moe_w_router[0].T, _pick(s_len, (256, 128)))
    n = b * s_len
    ne, _, f = moe_w_gate[0].shape
    tm = _pick(TOP_K * n, (1024, 512, 256, 128))
    pos, slot_token, tile_expert, tile_used = _route(idx, ne, tm)
    ys = _experts(h3.reshape(n, d), slot_token, tile_expert, tile_used, moe_w_gate[0], moe_w_up[0], moe_w_down[0],
                  tm, _pick(f, (512, 256, 128)))
    gates_t = jnp.swapaxes(gates, 1, 2).reshape(n, TOP_K)
    out = _combine(x3.reshape(n, d), ys, pos, gates_t, g2b, final_norm_w[None], _pick(s_len, (256, 128)))
    return out.reshape(b, s_len, d)
```

```python
import functools
import math

import jax
import jax.numpy as jnp
from jax import lax
from jax.experimental import pallas as pl
from jax.experimental.pallas import tpu as pltpu

F32 = jnp.float32
BF16 = jnp.bfloat16
FP8 = jnp.float8_e4m3fn

EPS = 1e-6
GRID_W = 64
ROPE_BASE = 10000.0
N_HEADS = 8
HEAD_DIM = 64
ROPE_FREQS = HEAD_DIM // 4
V_DIM = 2 * HEAD_DIM
QK_WIDTH = N_HEADS * 2 * HEAD_DIM
V_WIDTH = N_HEADS * V_DIM
MAP_FP8_WIDTH = 4 * HEAD_DIM
QK_FP8_WIDTH = 2 * MAP_FP8_WIDTH
K_PRESCALE = 0.25
V_ROWS = V_DIM + 16
KEY_TILES_PER_BLOCK = (5, 4, 3, 2)
SCORE_BUFFERS = 2
KEY_BLOCK_SPLIT = 1
STEPS_PER_TRIP = 1
DMA_ISSUE_UNROLL = 4
GM_CHUNK = 128
GM_GROUPS = 8
TOP_K = 2
NEG_BIG = -1e30
LOG2_E = math.log2(math.e)

LANES = 128
SUBLANES = 8
VMEM_LIMIT_BYTES = 56 * 1024 * 1024


def _cparams(semantics):
    return pltpu.CompilerParams(dimension_semantics=semantics, vmem_limit_bytes=VMEM_LIMIT_BYTES)


def _rms(x, w):
    return x * lax.rsqrt(jnp.mean(x * x, axis=-1, keepdims=True) + EPS) * w


def _silu(x):
    return x * jax.nn.sigmoid(x)


def _const_spec(shape):
    nd = len(shape)
    return pl.BlockSpec(shape, lambda *_: (0,) * nd, pipeline_mode=pl.Buffered(1))


def _ada_kernel(cond_ref, w_ref, b_ref, o_ref):
    s = _silu(cond_ref[...])
    o_ref[0] = jnp.dot(s, w_ref[0], preferred_element_type=F32, precision=lax.Precision.HIGHEST) + b_ref[0]


def _ada(cond, w_mod, b_mod):
    depth, d, n = w_mod.shape
    rows = cond.shape[0]
    tn = n // 4
    return pl.pallas_call(
        _ada_kernel,
        out_shape=jax.ShapeDtypeStruct((depth, rows, n), F32),
        grid=(depth, n // tn),
        in_specs=[
            pl.BlockSpec((rows, d), lambda l, j: (0, 0)),
            pl.BlockSpec((1, d, tn), lambda l, j: (l, 0, j)),
            pl.BlockSpec((1, 1, tn), lambda l, j: (l, 0, j)),
        ],
        out_specs=pl.BlockSpec((1, rows, tn), lambda l, j: (l, 0, j)),
        compiler_params=_cparams(("arbitrary", "arbitrary")),
        name="ada",
    )(cond, w_mod, b_mod.reshape(depth, 1, n))


def _split_fp8(x):
    hi = x.astype(FP8)
    return hi, (x - hi.astype(F32)).astype(FP8)


def _qkv_kernel(x_ref, ctx_ref, nw_ref, sh_ref, sc_ref, csh_ref, csc_ref, wqt_ref, wk_ref, wvt_ref,
                ck_ref, sk_ref, cq_ref, sq_ref, k_ref, qt_ref, vt_ref, *, n_ctx, ts):
    is_ctx = pl.program_id(1) < n_ctx
    xin = jnp.where(is_ctx, ctx_ref[0], x_ref[0])
    sh = jnp.where(is_ctx, csh_ref[0], sh_ref[0])
    sc = jnp.where(is_ctx, csc_ref[0], sc_ref[0])
    h = (_rms(xin, nw_ref[...]) * (1.0 + sc) + sh).astype(BF16)

    k = jnp.dot(h, wk_ref[...], preferred_element_type=F32)
    lane = lax.broadcasted_iota(jnp.int32, (ts, LANES), 1)
    lower = (lane % 32) < 16
    ck = ck_ref[...]
    sk = sk_ref[...]
    for j in range(N_HEADS):
        kb = k[:, j * LANES:(j + 1) * LANES]
        partner = jnp.where(lower, pltpu.roll(kb, LANES - 16, 1), pltpu.roll(kb, 16, 1))
        hi, lo = _split_fp8((kb * ck + partner * sk) * K_PRESCALE)
        pieces = []
        for c in range(2):
            h_c, l_c = hi[:, c * HEAD_DIM:(c + 1) * HEAD_DIM], lo[:, c * HEAD_DIM:(c + 1) * HEAD_DIM]
            pieces += [h_c, l_c, h_c, l_c]
        k_ref[0, :, j * QK_FP8_WIDTH:(j + 1) * QK_FP8_WIDTH] = jnp.concatenate(pieces, axis=1)

    nt = (((1,), (1,)), ((), ()))
    qt = lax.dot_general(wqt_ref[...], h, nt, preferred_element_type=F32)
    cq = cq_ref[...]
    sq = sq_ref[...]
    scale = HEAD_DIM ** -0.5 * LOG2_E / K_PRESCALE
    for g in range(QK_WIDTH // HEAD_DIM):
        blk = qt[g * HEAD_DIM:(g + 1) * HEAD_DIM]
        partner = jnp.concatenate([blk[16:32], blk[0:16], blk[48:64], blk[32:48]], axis=0)
        hi, lo = _split_fp8((blk * cq + partner * sq) * scale)
        qt_ref[0, 4 * g * HEAD_DIM:4 * (g + 1) * HEAD_DIM, :] = jnp.concatenate([hi, hi, lo, lo], axis=0)

    vt = lax.dot_general(wvt_ref[...], h, nt, preferred_element_type=F32)
    ones_row = (lax.broadcasted_iota(jnp.int32, (V_ROWS - V_DIM, ts), 0) == 0).astype(BF16)
    for hh in range(N_HEADS):
        vt_ref[0, hh, 0, 0:V_DIM, :] = vt[hh * V_DIM:(hh + 1) * V_DIM].astype(BF16)
        vt_ref[0, hh, 0, V_DIM:V_ROWS, :] = ones_row


def _rope_tables(s_len, ctx_len):
    pos = jnp.arange(s_len)
    row = (pos // GRID_W).astype(F32)
    col = (pos % GRID_W).astype(F32)
    inv_freq = ROPE_BASE ** (-jnp.arange(ROPE_FREQS, dtype=F32) / ROPE_FREQS)
    ar = row[:, None] * inv_freq
    ac = col[:, None] * inv_freq
    cos64 = jnp.concatenate([jnp.cos(ar), jnp.cos(ar), jnp.cos(ac), jnp.cos(ac)], axis=-1)
    sin64 = jnp.concatenate([-jnp.sin(ar), jnp.sin(ar), -jnp.sin(ac), jnp.sin(ac)], axis=-1)
    ck = jnp.concatenate([jnp.ones((ctx_len, LANES), F32), jnp.tile(cos64, (1, 2))], axis=0)
    sk = jnp.concatenate([jnp.zeros((ctx_len, LANES), F32), jnp.tile(sin64, (1, 2))], axis=0)
    return ck, sk, cos64.T, sin64.T


def _qkv(x, ctx, nw, sh, sc, csh, csc, wqt, wk, wvt, ts):
    b, s_len, d = x.shape
    ctx_len = ctx.shape[1]
    assert s_len % ts == 0 and ctx_len % ts == 0
    n_ctx = ctx_len // ts
    n_lat = s_len // ts
    ktot = ctx_len + s_len
    sub = _pick(n_ctx + n_lat, KEY_TILES_PER_BLOCK)
    ck, sk, cq, sq = _rope_tables(s_len, ctx_len)

    def lat(t):
        return jnp.maximum(t - n_ctx, 0)

    row = lambda bb, t: (bb, 0, 0)
    return pl.pallas_call(
        functools.partial(_qkv_kernel, n_ctx=n_ctx, ts=ts),
        out_shape=(
            jax.ShapeDtypeStruct((b, ktot, N_HEADS * QK_FP8_WIDTH), FP8),
            jax.ShapeDtypeStruct((b, N_HEADS * QK_FP8_WIDTH, s_len), FP8),
            jax.ShapeDtypeStruct((b, N_HEADS, ktot // (ts * sub), V_ROWS, ts * sub), BF16),
        ),
        grid=(b, n_ctx + n_lat),
        in_specs=[
            pl.BlockSpec((1, ts, d), lambda bb, t: (bb, lat(t), 0)),
            pl.BlockSpec((1, ts, d), lambda bb, t: (bb, jnp.minimum(t, n_ctx - 1), 0)),
            _const_spec((1, d)),
            pl.BlockSpec((1, 1, d), row),
            pl.BlockSpec((1, 1, d), row),
            _const_spec((1, 1, d)),
            _const_spec((1, 1, d)),
            _const_spec((QK_WIDTH, d)),
            _const_spec((d, QK_WIDTH)),
            _const_spec((V_WIDTH, d)),
            pl.BlockSpec((ts, LANES), lambda bb, t: (t, 0)),
            pl.BlockSpec((ts, LANES), lambda bb, t: (t, 0)),
            pl.BlockSpec((HEAD_DIM, ts), lambda bb, t: (0, lat(t))),
            pl.BlockSpec((HEAD_DIM, ts), lambda bb, t: (0, lat(t))),
        ],
        out_specs=(
            pl.BlockSpec((1, ts, N_HEADS * QK_FP8_WIDTH), lambda bb, t: (bb, t, 0)),
            pl.BlockSpec((1, N_HEADS * QK_FP8_WIDTH, ts), lambda bb, t: (bb, 0, lat(t))),
            pl.BlockSpec((1, N_HEADS, 1, V_ROWS, ts), lambda bb, t: (bb, 0, t // sub, 0, t % sub)),
        ),
        compiler_params=_cparams(("arbitrary", "arbitrary")),
        name="qkv",
    )(x, ctx, nw, sh, sc, csh, csc, wqt, wk, wvt, ck, sk, cq, sq)


def _attn_kernel(lq1_ref, lk1_ref, lq2_ref, lk2_ref, subw_ref, q_ref, k_ref, v_ref, o_ref, acc_scr,
                 *bufs, tq, tkb, nb, split, lam_init):
    nbuf = len(bufs) // 2
    assert nbuf % split == 0
    tks = tkb // split
    nb = nb * split
    s_bufs, p_bufs = bufs[:nbuf], bufs[nbuf:]
    acc_scr[...] = jnp.zeros_like(acc_scr)

    def score_block(blk, slot):
        off = pl.multiple_of(blk * tks, tks)
        block_max = []
        for c in range(2):
            feats = slice(c * MAP_FP8_WIDTH, (c + 1) * MAP_FP8_WIDTH)
            s = jnp.dot(k_ref[0, pl.ds(off, tks), feats], q_ref[0, feats, :], preferred_element_type=F32)
            s_bufs[slot][:, c * tq:(c + 1) * tq] = s
            block_max.append(jnp.max(s, axis=0, keepdims=True))
        return jnp.concatenate(block_max, axis=1)

    def softmax(slot, m, block_max):
        m_new = jnp.maximum(m, block_max)
        p_bufs[slot][...] = jnp.exp2(s_bufs[slot][...] - m_new).astype(BF16)
        return m_new, jnp.exp2(m - m_new)

    def values(j, slot, alpha):
        part = slot % split
        vt = v_ref[0, 0, j // split, :, part * tks:(part + 1) * tks]
        pv = jnp.dot(vt, p_bufs[slot][...], preferred_element_type=F32)
        acc_scr[...] = acc_scr[...] * alpha + pv

    def step(j, slot, m, alpha, block_max):
        next_max = score_block(j + 2, (slot + 2) % nbuf)
        m, alpha_next = softmax((slot + 1) % nbuf, m, block_max)
        values(j, slot, alpha)
        return m, alpha_next, next_max

    m, alpha = softmax(0, jnp.full((1, 2 * tq), NEG_BIG, F32), score_block(0, 0))
    block_max = score_block(1, 1) if nb > 1 else None
    n_full = max(nb - 2, 0)

    per_trip = nbuf * STEPS_PER_TRIP

    def unrolled(i, carry):
        for r in range(per_trip):
            carry = step(per_trip * i + r, r % nbuf, *carry)
        return carry

    if n_full:
        m, alpha, block_max = lax.fori_loop(0, n_full // per_trip, unrolled, (m, alpha, block_max))
    for j in range(per_trip * (n_full // per_trip), n_full):
        m, alpha, block_max = step(j, j % nbuf, m, alpha, block_max)
    if nb > 1:
        m, alpha_next = softmax((nb - 1) % nbuf, m, block_max)
        values(nb - 2, (nb - 2) % nbuf, alpha)
        alpha = alpha_next
    values(nb - 1, (nb - 1) % nbuf, alpha)

    lam = (jnp.exp(jnp.sum(lq1_ref[...] * lk1_ref[...], axis=-1, keepdims=True))
           - jnp.exp(jnp.sum(lq2_ref[...] * lk2_ref[...], axis=-1, keepdims=True)) + lam_init)
    acc = acc_scr[0:V_DIM, :]
    inv = 1.0 / acc_scr[V_DIM:V_DIM + 1, :]
    o = acc[:, 0:tq] * inv[:, 0:tq] - lam * (acc[:, tq:2 * tq] * inv[:, tq:2 * tq])
    y = o * lax.rsqrt(jnp.mean(o * o, axis=0, keepdims=True) + EPS) * subw_ref[...]
    y = y * (1.0 - lam_init)
    o_ref[0] = y.T.astype(BF16)


def _attention(qt, k_all, vt5, lq1, lk1, lq2, lk2, subw, lam_init, tq):
    b, _, s_len = qt.shape
    ktot = k_all.shape[1]
    nb, tkb = vt5.shape[2], vt5.shape[4]
    assert s_len % tq == 0 and nb * tkb == ktot and vt5.shape[3] == V_ROWS
    split = KEY_BLOCK_SPLIT if tkb % (KEY_BLOCK_SPLIT * LANES) == 0 else 1
    tks = tkb // split
    vec = _const_spec((1, HEAD_DIM))
    return pl.pallas_call(
        functools.partial(_attn_kernel, tq=tq, tkb=tkb, nb=nb, split=split, lam_init=lam_init),
        out_shape=jax.ShapeDtypeStruct((b, s_len, V_WIDTH), BF16),
        grid=(b, N_HEADS, s_len // tq),
        in_specs=[
            vec, vec, vec, vec,
            _const_spec((V_DIM, 1)),
            pl.BlockSpec((1, QK_FP8_WIDTH, tq), lambda bb, hh, i: (bb, hh, i)),
            pl.BlockSpec((1, ktot, QK_FP8_WIDTH), lambda bb, hh, i: (bb, 0, hh)),
            pl.BlockSpec((1, 1, nb, V_ROWS, tkb), lambda bb, hh, i: (bb, hh, 0, 0, 0)),
        ],
        out_specs=pl.BlockSpec((1, tq, V_DIM), lambda bb, hh, i: (bb, i, hh)),
        scratch_shapes=[
            pltpu.VMEM((V_ROWS, 2 * tq), F32),
        ] + [pltpu.VMEM((tks, 2 * tq), F32)] * SCORE_BUFFERS + [pltpu.VMEM((tks, 2 * tq), BF16)] * SCORE_BUFFERS,
        compiler_params=_cparams(("arbitrary", "arbitrary", "arbitrary")),
        name="attention",
    )(lq1, lk1, lq2, lk2, subw, qt, k_all, vt5)


def _ffn_kernel(x_ref, o_ref, wo_ref, g1_ref, nw_ref, sh_ref, sc_ref, g2_ref, wg_ref, wu_ref, wd_ref, out_ref):
    y = jnp.dot(o_ref[0], wo_ref[...], preferred_element_type=F32)
    x1 = x_ref[0] + g1_ref[0] * y
    h = (_rms(x1, nw_ref[...]) * (1.0 + sc_ref[0]) + sh_ref[0]).astype(BF16)
    a = jnp.dot(h, wg_ref[...], preferred_element_type=F32)
    u = jnp.dot(h, wu_ref[...], preferred_element_type=F32)
    z = (_silu(a) * u).astype(BF16)
    out_ref[0] = x1 + g2_ref[0] * jnp.dot(z, wd_ref[...], preferred_element_type=F32)


def _ffn(x, o, wo, g1, nw, sh, sc, g2, wg, wu, wd, ts):
    b, s_len, d = x.shape
    f = wg.shape[1]
    row = lambda bb, t: (bb, 0, 0)
    tile = lambda bb, t: (bb, t, 0)
    vec = pl.BlockSpec((1, 1, d), row)
    return pl.pallas_call(
        _ffn_kernel,
        out_shape=jax.ShapeDtypeStruct((b, s_len, d), F32),
        grid=(b, s_len // ts),
        in_specs=[
            pl.BlockSpec((1, ts, d), tile),
            pl.BlockSpec((1, ts, V_WIDTH), tile),
            _const_spec((V_WIDTH, d)),
            vec, _const_spec((1, d)), vec, vec, vec,
            _const_spec((d, f)), _const_spec((d, f)), _const_spec((f, d)),
        ],
        out_specs=pl.BlockSpec((1, ts, d), tile),
        compiler_params=_cparams(("arbitrary", "arbitrary")),
        name="ffn",
    )(x, o, wo, g1, nw, sh, sc, g2, wg, wu, wd)


def _gmlp_kernel(x_ref, nw1_ref, sh1_ref, sc1_ref, g1_ref, win_ref, vnw_ref, vnb_ref, ws_ref, bs_ref, wout_ref,
                 nw2_ref, sh2_ref, sc2_ref, wrt_ref, x3_ref, h3_ref, idx_ref, gate_ref, t_scr, *, ts, half):
    x2 = x_ref[0]
    h = (_rms(x2, nw1_ref[...]) * (1.0 + sc1_ref[0]) + sh1_ref[0]).astype(BF16)
    z = jnp.dot(h, win_ref[...], preferred_element_type=F32)
    z = 0.5 * z * (1.0 + lax.erf(z * (2.0 ** -0.5)))
    u = z[:, 0:half]
    v = z[:, half:2 * half]
    mu = jnp.mean(v, axis=-1, keepdims=True)
    vc = v - mu
    v = vc * lax.rsqrt(jnp.mean(vc * vc, axis=-1, keepdims=True) + EPS) * vnw_ref[...] + vnb_ref[...]
    vb = v.astype(BF16)
    gd = half // GM_GROUPS
    for n in range(ts // GM_CHUNK):
        rows = slice(n * GM_CHUNK, (n + 1) * GM_CHUNK)
        for g in range(GM_GROUPS):
            cols = slice(g * gd, (g + 1) * gd)
            s = jnp.dot(ws_ref[g], vb[rows, cols], preferred_element_type=F32) + bs_ref[g]
            t_scr[rows, cols] = (u[rows, cols] * s).astype(BF16)
    x3 = x2 + g1_ref[0] * jnp.dot(t_scr[...], wout_ref[...], preferred_element_type=F32)
    x3_ref[0] = x3

    h3 = _rms(x3, nw2_ref[...]) * (1.0 + sc2_ref[0]) + sh2_ref[0]
    h3_ref[0] = h3
    nt = (((1,), (1,)), ((), ()))
    logits = lax.dot_general(wrt_ref[...], h3, nt, preferred_element_type=F32,
                             precision=lax.Precision.HIGHEST)
    ne = logits.shape[0]
    eid = lax.broadcasted_iota(jnp.int32, logits.shape, 0)
    m1 = jnp.max(logits, axis=0, keepdims=True)
    i1 = jnp.min(jnp.where(logits == m1, eid, ne), axis=0, keepdims=True)
    rest = jnp.where(eid == i1, -jnp.inf, logits)
    m2 = jnp.max(rest, axis=0, keepdims=True)
    i2 = jnp.min(jnp.where(rest == m2, eid, ne), axis=0, keepdims=True)
    e2 = jnp.exp(m2 - m1)
    den = 1.0 + e2
    idx_ref[0] = jnp.concatenate([i1, i2], axis=0)
    gate_ref[0] = jnp.concatenate([1.0 / den, e2 / den], axis=0)


def _gmlp(x, nw1, sh1, sc1, g1, win, vnw, vnb, ws, bs, wout, nw2, sh2, sc2, wrt, ts):
    b, s_len, d = x.shape
    half = wout.shape[0]
    ne = wrt.shape[0]
    assert ts % GM_CHUNK == 0 and s_len % ts == 0
    row = lambda bb, t: (bb, 0, 0)
    tile = lambda bb, t: (bb, t, 0)
    vec = pl.BlockSpec((1, 1, d), row)
    return pl.pallas_call(
        functools.partial(_gmlp_kernel, ts=ts, half=half),
        out_shape=(
            jax.ShapeDtypeStruct((b, s_len, d), F32),
            jax.ShapeDtypeStruct((b, s_len, d), F32),
            jax.ShapeDtypeStruct((b, TOP_K, s_len), jnp.int32),
            jax.ShapeDtypeStruct((b, TOP_K, s_len), F32),
        ),
        grid=(b, s_len // ts),
        in_specs=[
            pl.BlockSpec((1, ts, d), tile),
            _const_spec((1, d)), vec, vec, vec,
            _const_spec((d, 2 * half)),
            _const_spec((1, half)), _const_spec((1, half)),
            _const_spec((GM_GROUPS, GM_CHUNK, GM_CHUNK)),
            _const_spec((GM_GROUPS, GM_CHUNK, 1)),
            _const_spec((half, d)),
            _const_spec((1, d)), vec, vec,
            _const_spec((ne, d)),
        ],
        out_specs=(
            pl.BlockSpec((1, ts, d), tile),
            pl.BlockSpec((1, ts, d), tile),
            pl.BlockSpec((1, TOP_K, ts), lambda bb, t: (bb, 0, t)),
            pl.BlockSpec((1, TOP_K, ts), lambda bb, t: (bb, 0, t)),
        ),
        scratch_shapes=[pltpu.VMEM((ts, half), BF16)],
        compiler_params=_cparams(("arbitrary", "arbitrary")),
        name="gmlp",
    )(x, nw1, sh1, sc1, g1, win, vnw, vnb, ws, bs, wout, nw2, sh2, sc2, wrt)


def _route(idx, n_experts, tm):
    b, k, s_len = idx.shape
    n = b * s_len
    e_flat = jnp.swapaxes(idx, 0, 1).reshape(k * n)
    onehot = (e_flat[:, None] == jnp.arange(n_experts, dtype=jnp.int32)[None, :]).astype(jnp.int32)
    csum = jnp.cumsum(onehot, axis=0)
    counts = csum[-1]
    rank = jnp.take_along_axis(csum, e_flat[:, None], axis=1)[:, 0] - 1
    padded = ((counts + tm - 1) // tm) * tm
    ends = jnp.cumsum(padded)
    starts = ends - padded
    pos = (starts[e_flat] + rank).astype(jnp.int32)
    n_tiles = (k * n) // tm + n_experts + 1
    tile_ids = jnp.arange(n_tiles, dtype=jnp.int32)
    used = tile_ids * tm < ends[-1]
    tile_src = jnp.where(used, tile_ids, ends[-1] // tm - 1).astype(jnp.int32)
    tile_expert = jnp.minimum(jnp.searchsorted(ends, tile_src * tm, side="right"), n_experts - 1).astype(jnp.int32)
    tokens = jnp.tile(jnp.arange(n, dtype=jnp.int32), k)
    slot_token = jnp.zeros((n_tiles * tm,), jnp.int32).at[pos].set(tokens, unique_indices=True)
    return pos.reshape(k, n), slot_token, tile_expert, used.astype(jnp.int32)


def _experts_kernel(exp_ref, used_ref, tok_ref, tokn_ref, h_ref, wg_ref, wu_ref, wd_ref, y_ref,
                    xbuf, xb_scr, acc_scr, sems, *, tm, per_step):
    j = pl.program_id(0)
    f = pl.program_id(1)
    n_rows = per_step * pl.num_programs(1)
    used = used_ref[j] > 0
    slot = j % 2

    def row_copy(t_ref, r, sl):
        return pltpu.make_async_copy(h_ref.at[pl.ds(t_ref[0, 0, r], 1)], xbuf.at[sl, pl.ds(r, 1)], sems.at[sl])

    @pl.when(f == 0)
    def _():
        @pl.when(j == 0)
        def _():
            def body(r, carry):
                row_copy(tok_ref, r, 0).start()
                return carry
            lax.fori_loop(0, n_rows, body, 0, unroll=DMA_ISSUE_UNROLL)

        @pl.when(used_ref[jnp.maximum(j - 1, 0)] > 0)
        def _():
            pltpu.make_async_copy(h_ref.at[pl.ds(0, n_rows)], xbuf.at[slot, pl.ds(0, n_rows)], sems.at[slot]).wait()
            xb_scr[...] = xbuf[slot, 0:tm].astype(BF16)

        acc_scr[...] = jnp.zeros_like(acc_scr)

    @pl.when(used)
    def _():
        for i in range(per_step):
            row_copy(tokn_ref, f * per_step + i, 1 - slot).start()
        xb = xb_scr[...]
        a = jnp.dot(xb, wg_ref[0].astype(BF16), preferred_element_type=F32)
        u = jnp.dot(xb, wu_ref[0].astype(BF16), preferred_element_type=F32)
        z = (_silu(a) * u).astype(BF16)
        acc_scr[...] += jnp.dot(z, wd_ref[0].astype(BF16), preferred_element_type=F32)

    @pl.when(f == pl.num_programs(1) - 1)
    def _():
        y_ref[...] = acc_scr[...]


def _experts(h, slot_token, tile_expert, tile_used, wg, wu, wd, tm, tf):
    n, d = h.shape
    ne, _, f = wg.shape
    nf = f // tf
    n_tiles = tile_expert.shape[0]
    per_step = -(-tm // (nf * SUBLANES)) * SUBLANES
    n_rows = per_step * nf
    assert slot_token.shape[0] == n_tiles * tm and f % tf == 0 and n_rows <= n
    tok3 = jnp.pad(slot_token.reshape(n_tiles, 1, tm), ((0, 0), (0, 0), (0, n_rows - tm)))

    def fchunk(j, fi, used):
        return jnp.where(used[j] > 0, fi, nf - 1)

    smem_tile = functools.partial(pl.BlockSpec, (1, 1, n_rows), memory_space=pltpu.SMEM)
    return pl.pallas_call(
        functools.partial(_experts_kernel, tm=tm, per_step=per_step),
        out_shape=jax.ShapeDtypeStruct((n_tiles * tm, d), F32),
        grid_spec=pltpu.PrefetchScalarGridSpec(
            num_scalar_prefetch=2,
            grid=(n_tiles, nf),
            in_specs=[
                smem_tile(lambda j, fi, exp, used: (j, 0, 0)),
                smem_tile(lambda j, fi, exp, used: (jnp.minimum(j + 1, n_tiles - 1), 0, 0)),
                pl.BlockSpec(memory_space=pl.ANY),
                pl.BlockSpec((1, d, tf), lambda j, fi, exp, used: (exp[j], 0, fchunk(j, fi, used))),
                pl.BlockSpec((1, d, tf), lambda j, fi, exp, used: (exp[j], 0, fchunk(j, fi, used))),
                pl.BlockSpec((1, tf, d), lambda j, fi, exp, used: (exp[j], fchunk(j, fi, used), 0)),
            ],
            out_specs=pl.BlockSpec((tm, d), lambda j, fi, exp, used: (j, 0)),
            scratch_shapes=[pltpu.VMEM((2, n_rows, d), F32), pltpu.VMEM((tm, d), BF16),
                            pltpu.VMEM((tm, d), F32), pltpu.SemaphoreType.DMA((2,))],
        ),
        compiler_params=_cparams(("arbitrary", "arbitrary")),
        name="moe_experts",
    )(tile_expert, tile_used, tok3, tok3, h, wg, wu, wd)


def _combine_kernel(pos_ref, posn_ref, x_ref, gate_ref, g2_ref, fw_ref, ys_ref, out_ref, ybuf, sems, *, rows):
    i = pl.program_id(0)
    slot = i % 2

    def issue(p_ref, sl):
        def body(r, carry):
            for k in range(TOP_K):
                pltpu.make_async_copy(ys_ref.at[pl.ds(p_ref[0, k, r], 1)], ybuf.at[sl, pl.ds(k * rows + r, 1)],
                                      sems.at[sl]).start()
            return carry
        lax.fori_loop(0, rows, body, 0, unroll=DMA_ISSUE_UNROLL)

    @pl.when(i == 0)
    def _():
        issue(pos_ref, 0)

    @pl.when(i + 1 < pl.num_programs(0))
    def _():
        issue(posn_ref, 1 - slot)

    pltpu.make_async_copy(ys_ref.at[pl.ds(0, TOP_K * rows)], ybuf.at[slot], sems.at[slot]).wait()
    g = gate_ref[...]
    y = g[:, 0:1] * ybuf[slot, 0:rows] + g[:, 1:2] * ybuf[slot, rows:2 * rows]
    out_ref[...] = _rms(x_ref[...] + g2_ref[0] * y, fw_ref[...])


def _combine(x, ys, pos, gates, g2, fw, rows):
    n, d = x.shape
    n_steps = n // rows
    steps_per_batch = n_steps // g2.shape[0]
    assert n % rows == 0 and n_steps % g2.shape[0] == 0
    pos3 = jnp.swapaxes(pos.reshape(TOP_K, n_steps, rows), 0, 1)
    smem_tile = functools.partial(pl.BlockSpec, (1, TOP_K, rows), memory_space=pltpu.SMEM)
    return pl.pallas_call(
        functools.partial(_combine_kernel, rows=rows),
        out_shape=jax.ShapeDtypeStruct((n, d), F32),
        grid=(n_steps,),
        in_specs=[
            smem_tile(lambda i: (i, 0, 0)),
            smem_tile(lambda i: (jnp.minimum(i + 1, n_steps - 1), 0, 0)),
            pl.BlockSpec((rows, d), lambda i: (i, 0)),
            pl.BlockSpec((rows, TOP_K), lambda i: (i, 0)),
            pl.BlockSpec((1, 1, d), lambda i: (i // steps_per_batch, 0, 0)),
            pl.BlockSpec((1, d), lambda i: (0, 0)),
            pl.BlockSpec(memory_space=pl.ANY),
        ],
        out_specs=pl.BlockSpec((rows, d), lambda i: (i, 0)),
        scratch_shapes=[pltpu.VMEM((2, TOP_K * rows, d), F32), pltpu.SemaphoreType.DMA((2,))],
        compiler_params=_cparams(("arbitrary",)),
        name="moe_combine",
    )(pos3, pos3, x, gates, g2, fw, ys)


def _pick(n, prefs):
    for p in prefs:
        if n % p == 0:
            return p
    return n


def kernel(x, c, ctx, c_ctx, w_mod, b_mod, norm1_w, norm2_w, final_norm_w, da_w_qkv, da_w_o, da_lambda_q1, da_lambda_k1, da_lambda_q2, da_lambda_k2, da_subln_w, gm_w_in, gm_vnorm_w, gm_vnorm_b, gm_w_s, gm_b_s, gm_w_out, ffn_w_gate, ffn_w_up, ffn_w_down, moe_w_router, moe_w_gate, moe_w_up, moe_w_down):
    b, s_len, d = x.shape
    ctx_len = ctx.shape[1]

    pad = (-(b + 1)) % 8
    cond = jnp.concatenate([c, c_ctx[None, :], jnp.zeros((pad, d), F32)], axis=0)
    mod = _ada(cond, w_mod, b_mod)

    def chunks(layer, r0, r1):
        m = mod[layer, r0:r1]
        return [m[:, None, i * d:(i + 1) * d] for i in range(6)]

    sh1, sc1, g1, sh2, sc2, g2 = chunks(0, 0, b)
    csh1, csc1 = chunks(0, b, b + 1)[:2]
    sh1b, sc1b, g1b, sh2b, sc2b, g2b = chunks(1, 0, b)

    ts = _pick(ctx_len, (256, 128))
    wq = da_w_qkv[0, :, 0:QK_WIDTH]
    wk = da_w_qkv[0, :, QK_WIDTH:2 * QK_WIDTH]
    wv = da_w_qkv[0, :, 2 * QK_WIDTH:]
    k_all, qt, vt5 = _qkv(x, ctx, norm1_w[0][None], sh1, sc1, csh1, csc1,
                          wq.T.astype(BF16), wk.astype(BF16), wv.T.astype(BF16), ts)
    lam_init = 0.8 - 0.6 * math.exp(-0.3 * 0)
    o = _attention(qt, k_all, vt5, da_lambda_q1[0][None], da_lambda_k1[0][None], da_lambda_q2[0][None],
                   da_lambda_k2[0][None], da_subln_w[0][:, None], lam_init, _pick(s_len, (512, 256, 128)))
    x2 = _ffn(x, o, da_w_o[0].astype(BF16), g1, norm2_w[0][None], sh2, sc2, g2,
              ffn_w_gate[0].astype(BF16), ffn_w_up[0].astype(BF16), ffn_w_down[0].astype(BF16),
              _pick(s_len, (256, 128)))

    x3, h3, idx, gates = _gmlp(x2, norm1_w[1][None], sh1b, sc1b, g1b, gm_w_in[0].astype(BF16),
                               gm_vnorm_w[0][None], gm_vnorm_b[0][None], gm_w_s[0].astype(BF16),
                               gm_b_s[0][:, :, None], gm_w_out[0].astype(BF16),
                               norm2_w[1][None], sh2b, sc2b, moe_w_router[0].T, _pick(s_len, (256, 128)))
    n = b * s_len
    ne, _, f = moe_w_gate[0].shape
    tm = _pick(TOP_K * n, (1024, 512, 256, 128))
    pos, slot_token, tile_expert, tile_used = _route(idx, ne, tm)
    ys = _experts(h3.reshape(n, d), slot_token, tile_expert, tile_used, moe_w_gate[0], moe_w_up[0], moe_w_down[0],
                  tm, _pick(f, (512, 256, 128)))
    gates_t = jnp.swapaxes(gates, 1, 2).reshape(n, TOP_K)
    out = _combine(x3.reshape(n, d), ys, pos, gates_t, g2b, final_norm_w[None], _pick(s_len, (256, 128)))
    return out.reshape(b, s_len, d)
```

```python
import functools
import math

import jax
import jax.numpy as jnp
from jax import lax
from jax.experimental import pallas as pl
from jax.experimental.pallas import tpu as pltpu

F32 = jnp.float32
BF16 = jnp.bfloat16
FP8 = jnp.float8_e4m3fn

EPS = 1e-6
GRID_W = 64
ROPE_BASE = 10000.0
N_HEADS = 8
HEAD_DIM = 64
ROPE_FREQS = HEAD_DIM // 4
V_DIM = 2 * HEAD_DIM
QK_WIDTH = N_HEADS * 2 * HEAD_DIM
V_WIDTH = N_HEADS * V_DIM
MAP_FP8_WIDTH = 4 * HEAD_DIM
QK_FP8_WIDTH = 2 * MAP_FP8_WIDTH
K_PRESCALE = 0.25
V_ROWS = V_DIM + 16
KEY_TILES_PER_BLOCK = (5, 4, 3, 2)
SCORE_BUFFERS = 2
KEY_BLOCK_SPLIT = 1
STEPS_PER_TRIP = 1
DMA_ISSUE_UNROLL = 4
GM_CHUNK = 128
GM_GROUPS = 8
TOP_K = 2
NEG_BIG = -1e30
LOG2_E = math.log2(math.e)

LANES = 128
SUBLANES = 8
VMEM_LIMIT_BYTES = 56 * 1024 * 1024


def _cparams(semantics):
    return pltpu.CompilerParams(dimension_semantics=semantics, vmem_limit_bytes=VMEM_LIMIT_BYTES)


def _rms(x, w):
    return x * lax.rsqrt(jnp.mean(x * x, axis=-1, keepdims=True) + EPS) * w


def _silu(x):
    return x * jax.nn.sigmoid(x)


def _const_spec(shape):
    nd = len(shape)
    return pl.BlockSpec(shape, lambda *_: (0,) * nd, pipeline_mode=pl.Buffered(1))


def _ada_kernel(cond_ref, w_ref, b_ref, o_ref):
    s = _silu(cond_ref[...])
    o_ref[0] = jnp.dot(s, w_ref[0], preferred_element_type=F32, precision=lax.Precision.HIGHEST) + b_ref[0]


def _ada(cond, w_mod, b_mod):
    depth, d, n = w_mod.shape
    rows = cond.shape[0]
    tn = n // 4
    return pl.pallas_call(
        _ada_kernel,
        out_shape=jax.ShapeDtypeStruct((depth, rows, n), F32),
        grid=(depth, n // tn),
        in_specs=[
            pl.BlockSpec((rows, d), lambda l, j: (0, 0)),
            pl.BlockSpec((1, d, tn), lambda l, j: (l, 0, j)),
            pl.BlockSpec((1, 1, tn), lambda l, j: (l, 0, j)),
        ],
        out_specs=pl.BlockSpec((1, rows, tn), lambda l, j: (l, 0, j)),
        compiler_params=_cparams(("arbitrary", "arbitrary")),
        name="ada",
    )(cond, w_mod, b_mod.reshape(depth, 1, n))


def _split_fp8(x):
    hi = x.astype(FP8)
    return hi, (x - hi.astype(F32)).astype(FP8)


def _qkv_kernel(x_ref, ctx_ref, nw_ref, sh_ref, sc_ref, csh_ref, csc_ref, wqt_ref, wk_ref, wvt_ref,
                ck_ref, sk_ref, cq_ref, sq_ref, k_ref, qt_ref, vt_ref, *, n_ctx, ts):
    is_ctx = pl.program_id(1) < n_ctx
    xin = jnp.where(is_ctx, ctx_ref[0], x_ref[0])
    sh = jnp.where(is_ctx, csh_ref[0], sh_ref[0])
    sc = jnp.where(is_ctx, csc_ref[0], sc_ref[0])
    h = (_rms(xin, nw_ref[...]) * (1.0 + sc) + sh).astype(BF16)

    k = jnp.dot(h, wk_ref[...], preferred_element_type=F32)
    lane = lax.broadcasted_iota(jnp.int32, (ts, LANES), 1)
    lower = (lane % 32) < 16
    ck = ck_ref[...]
    sk = sk_ref[...]
    for j in range(N_HEADS):
        kb = k[:, j * LANES:(j + 1) * LANES]
        partner = jnp.where(lower, pltpu.roll(kb, LANES - 16, 1), pltpu.roll(kb, 16, 1))
        hi, lo = _split_fp8((kb * ck + partner * sk) * K_PRESCALE)
        pieces = []
        for c in range(2):
            h_c, l_c = hi[:, c * HEAD_DIM:(c + 1) * HEAD_DIM], lo[:, c * HEAD_DIM:(c + 1) * HEAD_DIM]
            pieces += [h_c, l_c, h_c, l_c]
        k_ref[0, :, j * QK_FP8_WIDTH:(j + 1) * QK_FP8_WIDTH] = jnp.concatenate(pieces, axis=1)

    nt = (((1,), (1,)), ((), ()))
    qt = lax.dot_general(wqt_ref[...], h, nt, preferred_element_type=F32)
    cq = cq_ref[...]
    sq = sq_ref[...]
    scale = HEAD_DIM ** -0.5 * LOG2_E / K_PRESCALE
    for g in range(QK_WIDTH // HEAD_DIM):
        blk = qt[g * HEAD_DIM:(g + 1) * HEAD_DIM]
        partner = jnp.concatenate([blk[16:32], blk[0:16], blk[48:64], blk[32:48]], axis=0)
        hi, lo = _split_fp8((blk * cq + partner * sq) * scale)
        qt_ref[0, 4 * g * HEAD_DIM:4 * (g + 1) * HEAD_DIM, :] = jnp.concatenate([hi, hi, lo, lo], axis=0)

    vt = lax.dot_general(wvt_ref[...], h, nt, preferred_element_type=F32)
    ones_row = (lax.broadcasted_iota(jnp.int32, (V_ROWS - V_DIM, ts), 0) == 0).astype(BF16)
    for hh in range(N_HEADS):
        vt_ref[0, hh, 0, 0:V_DIM, :] = vt[hh * V_DIM:(hh + 1) * V_DIM].astype(BF16)
        vt_ref[0, hh, 0, V_DIM:V_ROWS, :] = ones_row


def _rope_tables(s_len, ctx_len):
    pos = jnp.arange(s_len)
    row = (pos // GRID_W).astype(F32)
    col = (pos % GRID_W).astype(F32)
    inv_freq = ROPE_BASE ** (-jnp.arange(ROPE_FREQS, dtype=F32) / ROPE_FREQS)
    ar = row[:, None] * inv_freq
    ac = col[:, None] * inv_freq
    cos64 = jnp.concatenate([jnp.cos(ar), jnp.cos(ar), jnp.cos(ac), jnp.cos(ac)], axis=-1)
    sin64 = jnp.concatenate([-jnp.sin(ar), jnp.sin(ar), -jnp.sin(ac), jnp.sin(ac)], axis=-1)
    ck = jnp.concatenate([jnp.ones((ctx_len, LANES), F32), jnp.tile(cos64, (1, 2))], axis=0)
    sk = jnp.concatenate([jnp.zeros((ctx_len, LANES), F32), jnp.tile(sin64, (1, 2))], axis=0)
    return ck, sk, cos64.T, sin64.T


def _qkv(x, ctx, nw, sh, sc, csh, csc, wqt, wk, wvt, ts):
    b, s_len, d = x.shape
    ctx_len = ctx.shape[1]
    assert s_len % ts == 0 and ctx_len % ts == 0
    n_ctx = ctx_len // ts
    n_lat = s_len // ts
    ktot = ctx_len + s_len
    sub = _pick(n_ctx + n_lat, KEY_TILES_PER_BLOCK)
    ck, sk, cq, sq = _rope_tables(s_len, ctx_len)

    def lat(t):
        return jnp.maximum(t - n_ctx, 0)

    row = lambda bb, t: (bb, 0, 0)
    return pl.pallas_call(
        functools.partial(_qkv_kernel, n_ctx=n_ctx, ts=ts),
        out_shape=(
            jax.ShapeDtypeStruct((b, ktot, N_HEADS * QK_FP8_WIDTH), FP8),
            jax.ShapeDtypeStruct((b, N_HEADS * QK_FP8_WIDTH, s_len), FP8),
            jax.ShapeDtypeStruct((b, N_HEADS, ktot // (ts * sub), V_ROWS, ts * sub), BF16),
        ),
        grid=(b, n_ctx + n_lat),
        in_specs=[
            pl.BlockSpec((1, ts, d), lambda bb, t: (bb, lat(t), 0)),
            pl.BlockSpec((1, ts, d), lambda bb, t: (bb, jnp.minimum(t, n_ctx - 1), 0)),
            _const_spec((1, d)),
            pl.BlockSpec((1, 1, d), row),
            pl.BlockSpec((1, 1, d), row),
            _const_spec((1, 1, d)),
            _const_spec((1, 1, d)),
            _const_spec((QK_WIDTH, d)),
            _const_spec((d, QK_WIDTH)),
            _const_spec((V_WIDTH, d)),
            pl.BlockSpec((ts, LANES), lambda bb, t: (t, 0)),
            pl.BlockSpec((ts, LANES), lambda bb, t: (t, 0)),
            pl.BlockSpec((HEAD_DIM, ts), lambda bb, t: (0, lat(t))),
            pl.BlockSpec((HEAD_DIM, ts), lambda bb, t: (0, lat(t))),
        ],
        out_specs=(
            pl.BlockSpec((1, ts, N_HEADS * QK_FP8_WIDTH), lambda bb, t: (bb, t, 0)),
            pl.BlockSpec((1, N_HEADS * QK_FP8_WIDTH, ts), lambda bb, t: (bb, 0, lat(t))),
            pl.BlockSpec((1, N_HEADS, 1, V_ROWS, ts), lambda bb, t: (bb, 0, t // sub, 0, t % sub)),
        ),
        compiler_params=_cparams(("arbitrary", "arbitrary")),
        name="qkv",
    )(x, ctx, nw, sh, sc, csh, csc, wqt, wk, wvt, ck, sk, cq, sq)


def _attn_kernel(lq1_ref, lk1_ref, lq2_ref, lk2_ref, subw_ref, q_ref, k_ref, v_ref, o_ref, acc_scr,
                 *bufs, tq, tkb, nb, split, lam_init):
    nbuf = len(bufs) // 2
    assert nbuf % split == 0
    tks = tkb // split
    nb = nb * split
    s_bufs, p_bufs = bufs[:nbuf], bufs[nbuf:]
    acc_scr[...] = jnp.zeros_like(acc_scr)

    def score_block(blk, slot):
        off = pl.multiple_of(blk * tks, tks)
        block_max = []
        for c in range(2):
            feats = slice(c * MAP_FP8_WIDTH, (c + 1) * MAP_FP8_WIDTH)
            s = jnp.dot(k_ref[0, pl.ds(off, tks), feats], q_ref[0, feats, :], preferred_element_type=F32)
            s_bufs[slot][:, c * tq:(c + 1) * tq] = s
            block_max.append(jnp.max(s, axis=0, keepdims=True))
        return jnp.concatenate(block_max, axis=1)

    def softmax(slot, m, block_max):
        m_new = jnp.maximum(m, block_max)
        p_bufs[slot][...] = jnp.exp2(s_bufs[slot][...] - m_new).astype(BF16)
        return m_new, jnp.exp2(m - m_new)

    def values(j, slot, alpha):
        part = slot % split
        vt = v_ref[0, 0, j // split, :, part * tks:(part + 1) * tks]
        pv = jnp.dot(vt, p_bufs[slot][...], preferred_element_type=F32)
        acc_scr[...] = acc_scr[...] * alpha + pv

    def step(j, slot, m, alpha, block_max):
        next_max = score_block(j + 2, (slot + 2) % nbuf)
        m, alpha_next = softmax((slot + 1) % nbuf, m, block_max)
        values(j, slot, alpha)
        return m, alpha_next, next_max

    m, alpha = softmax(0, jnp.full((1, 2 * tq), NEG_BIG, F32), score_block(0, 0))
    block_max = score_block(1, 1) if nb > 1 else None
    n_full = max(nb - 2, 0)

    per_trip = nbuf * STEPS_PER_TRIP

    def unrolled(i, carry):
        for r in range(per_trip):
            carry = step(per_trip * i + r, r % nbuf, *carry)
        return carry

    if n_full:
        m, alpha, block_max = lax.fori_loop(0, n_full // per_trip, unrolled, (m, alpha, block_max))
    for j in range(per_trip * (n_full // per_trip), n_full):
        m, alpha, block_max = step(j, j % nbuf, m, alpha, block_max)
    if nb > 1:
        m, alpha_next = softmax((nb - 1) % nbuf, m, block_max)
        values(nb - 2, (nb - 2) % nbuf, alpha)
        alpha = alpha_next
    values(nb - 1, (nb - 1) % nbuf, alpha)

    lam = (jnp.exp(jnp.sum(lq1_ref[...] * lk1_ref[...], axis=-1, keepdims=True))
           - jnp.exp(jnp.sum(lq2_ref[...] * lk2_ref[...], axis=-1, keepdims=True)) + lam_init)
    acc = acc_scr[0:V_DIM, :]
    inv = 1.0 / acc_scr[V_DIM:V_DIM + 1, :]
    o = acc[:, 0:tq] * inv[:, 0:tq] - lam * (acc[:, tq:2 * tq] * inv[:, tq:2 * tq])
    y = o * lax.rsqrt(jnp.mean(o * o, axis=0, keepdims=True) + EPS) * subw_ref[...]
    y = y * (1.0 - lam_init)
    o_ref[0] = y.T.astype(BF16)


def _attention(qt, k_all, vt5, lq1, lk1, lq2, lk2, subw, lam_init, tq):
    b, _, s_len = qt.shape
    ktot = k_all.shape[1]
    nb, tkb = vt5.shape[2], vt5.shape[4]
    assert s_len % tq == 0 and nb * tkb == ktot and vt5.shape[3] == V_ROWS
    split = KEY_BLOCK_SPLIT if tkb % (KEY_BLOCK_SPLIT * LANES) == 0 else 1
    tks = tkb // split
    vec = _const_spec((1, HEAD_DIM))
    return pl.pallas_call(
        functools.partial(_attn_kernel, tq=tq, tkb=tkb, nb=nb, split=split, lam_init=lam_init),
        out_shape=jax.ShapeDtypeStruct((b, s_len, V_WIDTH), BF16),
        grid=(b, N_HEADS, s_len // tq),
        in_specs=[
            vec, vec, vec, vec,
            _const_spec((V_DIM, 1)),
            pl.BlockSpec((1, QK_FP8_WIDTH, tq), lambda bb, hh, i: (bb, hh, i)),
            pl.BlockSpec((1, ktot, QK_FP8_WIDTH), lambda bb, hh, i: (bb, 0, hh)),
            pl.BlockSpec((1, 1, nb, V_ROWS, tkb), lambda bb, hh, i: (bb, hh, 0, 0, 0)),
        ],
        out_specs=pl.BlockSpec((1, tq, V_DIM), lambda bb, hh, i: (bb, i, hh)),
        scratch_shapes=[
            pltpu.VMEM((V_ROWS, 2 * tq), F32),
        ] + [pltpu.VMEM((tks, 2 * tq), F32)] * SCORE_BUFFERS + [pltpu.VMEM((tks, 2 * tq), BF16)] * SCORE_BUFFERS,
        compiler_params=_cparams(("arbitrary", "arbitrary", "arbitrary")),
        name="attention",
    )(lq1, lk1, lq2, lk2, subw, qt, k_all, vt5)


def _ffn_kernel(x_ref, o_ref, wo_ref, g1_ref, nw_ref, sh_ref, sc_ref, g2_ref, wg_ref, wu_ref, wd_ref, out_ref):
    y = jnp.dot(o_ref[0], wo_ref[...], preferred_element_type=F32)
    x1 = x_ref[0] + g1_ref[0] * y
    h = (_rms(x1, nw_ref[...]) * (1.0 + sc_ref[0]) + sh_ref[0]).astype(BF16)
    a = jnp.dot(h, wg_ref[...], preferred_element_type=F32)
    u = jnp.dot(h, wu_ref[...], preferred_element_type=F32)
    z = (_silu(a) * u).astype(BF16)
    out_ref[0] = x1 + g2_ref[0] * jnp.dot(z, wd_ref[...], preferred_element_type=F32)


def _ffn(x, o, wo, g1, nw, sh, sc, g2, wg, wu, wd, ts):
    b, s_len, d = x.shape
    f = wg.shape[1]
    row = lambda bb, t: (bb, 0, 0)
    tile = lambda bb, t: (bb, t, 0)
    vec = pl.BlockSpec((1, 1, d), row)
    return pl.pallas_call(
        _ffn_kernel,
        out_shape=jax.ShapeDtypeStruct((b, s_len, d), F32),
        grid=(b, s_len // ts),
        in_specs=[
            pl.BlockSpec((1, ts, d), tile),
            pl.BlockSpec((1, ts, V_WIDTH), tile),
            _const_spec((V_WIDTH, d)),
            vec, _const_spec((1, d)), vec, vec, vec,
            _const_spec((d, f)), _const_spec((d, f)), _const_spec((f, d)),
        ],
        out_specs=pl.BlockSpec((1, ts, d), tile),
        compiler_params=_cparams(("arbitrary", "arbitrary")),
        name="ffn",
    )(x, o, wo, g1, nw, sh, sc, g2, wg, wu, wd)


def _gmlp_kernel(x_ref, nw1_ref, sh1_ref, sc1_ref, g1_ref, win_ref, vnw_ref, vnb_ref, ws_ref, bs_ref, wout_ref,
                 nw2_ref, sh2_ref, sc2_ref, wrt_ref, x3_ref, h3_ref, idx_ref, gate_ref, t_scr, *, ts, half):
    x2 = x_ref[0]
    h = (_rms(x2, nw1_ref[...]) * (1.0 + sc1_ref[0]) + sh1_ref[0]).astype(BF16)
    z = jnp.dot(h, win_ref[...], preferred_element_type=F32)
    z = 0.5 * z * (1.0 + lax.erf(z * (2.0 ** -0.5)))
    u = z[:, 0:half]
    v = z[:, half:2 * half]
    mu = jnp.mean(v, axis=-1, keepdims=True)
    vc = v - mu
    v = vc * lax.rsqrt(jnp.mean(vc * vc, axis=-1, keepdims=True) + EPS) * vnw_ref[...] + vnb_ref[...]
    vb = v.astype(BF16)
    gd = half // GM_GROUPS
    for n in range(ts // GM_CHUNK):
        rows = slice(n * GM_CHUNK, (n + 1) * GM_CHUNK)
        for g in range(GM_GROUPS):
            cols = slice(g * gd, (g + 1) * gd)
            s = jnp.dot(ws_ref[g], vb[rows, cols], preferred_element_type=F32) + bs_ref[g]
            t_scr[rows, cols] = (u[rows, cols] * s).astype(BF16)
    x3 = x2 + g1_ref[0] * jnp.dot(t_scr[...], wout_ref[...], preferred_element_type=F32)
    x3_ref[0] = x3

    h3 = _rms(x3, nw2_ref[...]) * (1.0 + sc2_ref[0]) + sh2_ref[0]
    h3_ref[0] = h3
    nt = (((1,), (1,)), ((), ()))
    logits = lax.dot_general(wrt_ref[...], h3, nt, preferred_element_type=F32,
                             precision=lax.Precision.HIGHEST)
    ne = logits.shape[0]
    eid = lax.broadcasted_iota(jnp.int32, logits.shape, 0)
    m1 = jnp.max(logits, axis=0, keepdims=True)
    i1 = jnp.min(jnp.where(logits == m1, eid, ne), axis=0, keepdims=True)
    rest = jnp.where(eid == i1, -jnp.inf, logits)
    m2 = jnp.max(rest, axis=0, keepdims=True)
    i2 = jnp.min(jnp.where(rest == m2, eid, ne), axis=0, keepdims=True)
    e2 = jnp.exp(m2 - m1)
    den = 1.0 + e2
    idx_ref[0] = jnp.concatenate([i1, i2], axis=0)
    gate_ref[0] = jnp.concatenate([1.0 / den, e2 / den], axis=0)


def _gmlp(x, nw1, sh1, sc1, g1, win, vnw, vnb, ws, bs, wout, nw2, sh2, sc2, wrt, ts):
    b, s_len, d = x.shape
    half = wout.shape[0]
    ne = wrt.shape[0]
    assert ts % GM_CHUNK == 0 and s_len % ts == 0
    row = lambda bb, t: (bb, 0, 0)
    tile = lambda bb, t: (bb, t, 0)
    vec = pl.BlockSpec((1, 1, d), row)
    return pl.pallas_call(
        functools.partial(_gmlp_kernel, ts=ts, half=half),
        out_shape=(
            jax.ShapeDtypeStruct((b, s_len, d), F32),
            jax.ShapeDtypeStruct((b, s_len, d), F32),
            jax.ShapeDtypeStruct((b, TOP_K, s_len), jnp.int32),
            jax.ShapeDtypeStruct((b, TOP_K, s_len), F32),
        ),
        grid=(b, s_len // ts),
        in_specs=[
            pl.BlockSpec((1, ts, d), tile),
            _const_spec((1, d)), vec, vec, vec,
            _const_spec((d, 2 * half)),
            _const_spec((1, half)), _const_spec((1, half)),
            _const_spec((GM_GROUPS, GM_CHUNK, GM_CHUNK)),
            _const_spec((GM_GROUPS, GM_CHUNK, 1)),
            _const_spec((half, d)),
            _const_spec((1, d)), vec, vec,
            _const_spec((ne, d)),
        ],
        out_specs=(
            pl.BlockSpec((1, ts, d), tile),
            pl.BlockSpec((1, ts, d), tile),
            pl.BlockSpec((1, TOP_K, ts), lambda bb, t: (bb, 0, t)),
            pl.BlockSpec((1, TOP_K, ts), lambda bb, t: (bb, 0, t)),
        ),
        scratch_shapes=[pltpu.VMEM((ts, half), BF16)],
        compiler_params=_cparams(("arbitrary", "arbitrary")),
        name="gmlp",
    )(x, nw1, sh1, sc1, g1, win, vnw, vnb, ws, bs, wout, nw2, sh2, sc2, wrt)


def _route(idx, n_experts, tm):
    b, k, s_len = idx.shape
    n = b * s_len
    e_flat = jnp.swapaxes(idx, 0, 1).reshape(k * n)
    onehot = (e_flat[:, None] == jnp.arange(n_experts, dtype=jnp.int32)[None, :]).astype(jnp.int32)
    csum = jnp.cumsum(onehot, axis=0)
    counts = csum[-1]
    rank = jnp.take_along_axis(csum, e_flat[:, None], axis=1)[:, 0] - 1
    padded = ((counts + tm - 1) // tm) * tm
    ends = jnp.cumsum(padded)
    starts = ends - padded
    pos = (starts[e_flat] + rank).astype(jnp.int32)
    n_tiles = (k * n) // tm + n_experts + 1
    tile_ids = jnp.arange(n_tiles, dtype=jnp.int32)
    used = tile_ids * tm < ends[-1]
    tile_src = jnp.where(used, tile_ids, ends[-1] // tm - 1).astype(jnp.int32)
    tile_expert = jnp.minimum(jnp.searchsorted(ends, tile_src * tm, side="right"), n_experts - 1).astype(jnp.int32)
    tokens = jnp.tile(jnp.arange(n, dtype=jnp.int32), k)
    slot_token = jnp.zeros((n_tiles * tm,), jnp.int32).at[pos].set(tokens, unique_indices=True)
    return pos.reshape(k, n), slot_token, tile_expert, used.astype(jnp.int32)


def _experts_kernel(exp_ref, used_ref, tok_ref, tokn_ref, h_ref, wg_ref, wu_ref, wd_ref, y_ref,
                    xbuf, xb_scr, acc_scr, sems, *, tm, per_step):
    j = pl.program_id(0)
    f = pl.program_id(1)
    n_rows = per_step * pl.num_programs(1)
    used = used_ref[j] > 0
    slot = j % 2

    def row_copy(t_ref, r, sl):
        return pltpu.make_async_copy(h_ref.at[pl.ds(t_ref[0, 0, r], 1)], xbuf.at[sl, pl.ds(r, 1)], sems.at[sl])

    @pl.when(f == 0)
    def _():
        @pl.when(j == 0)
        def _():
            def body(r, carry):
                row_copy(tok_ref, r, 0).start()
                return carry
            lax.fori_loop(0, n_rows, body, 0, unroll=DMA_ISSUE_UNROLL)

        @pl.when(used_ref[jnp.maximum(j - 1, 0)] > 0)
        def _():
            pltpu.make_async_copy(h_ref.at[pl.ds(0, n_rows)], xbuf.at[slot, pl.ds(0, n_rows)], sems.at[slot]).wait()
            xb_scr[...] = xbuf[slot, 0:tm].astype(BF16)

        acc_scr[...] = jnp.zeros_like(acc_scr)

    @pl.when(used)
    def _():
        for i in range(per_step):
            row_copy(tokn_ref, f * per_step + i, 1 - slot).start()
        xb = xb_scr[...]
        a = jnp.dot(xb, wg_ref[0].astype(BF16), preferred_element_type=F32)
        u = jnp.dot(xb, wu_ref[0].astype(BF16), preferred_element_type=F32)
        z = (_silu(a) * u).astype(BF16)
        acc_scr[...] += jnp.dot(z, wd_ref[0].astype(BF16), preferred_element_type=F32)

    @pl.when(f == pl.num_programs(1) - 1)
    def _():
        y_ref[...] = acc_scr[...]


def _experts(h, slot_token, tile_expert, tile_used, wg, wu, wd, tm, tf):
    n, d = h.shape
    ne, _, f = wg.shape
    nf = f // tf
    n_tiles = tile_expert.shape[0]
    per_step = -(-tm // (nf * SUBLANES)) * SUBLANES
    n_rows = per_step * nf
    assert slot_token.shape[0] == n_tiles * tm and f % tf == 0 and n_rows <= n
    tok3 = jnp.pad(slot_token.reshape(n_tiles, 1, tm), ((0, 0), (0, 0), (0, n_rows - tm)))

    def fchunk(j, fi, used):
        return jnp.where(used[j] > 0, fi, nf - 1)

    smem_tile = functools.partial(pl.BlockSpec, (1, 1, n_rows), memory_space=pltpu.SMEM)
    return pl.pallas_call(
        functools.partial(_experts_kernel, tm=tm, per_step=per_step),
        out_shape=jax.ShapeDtypeStruct((n_tiles * tm, d), F32),
        grid_spec=pltpu.PrefetchScalarGridSpec(
            num_scalar_prefetch=2,
            grid=(n_tiles, nf),
            in_specs=[
                smem_tile(lambda j, fi, exp, used: (j, 0, 0)),
                smem_tile(lambda j, fi, exp, used: (jnp.minimum(j + 1, n_tiles - 1), 0, 0)),
                pl.BlockSpec(memory_space=pl.ANY),
                pl.BlockSpec((1, d, tf), lambda j, fi, exp, used: (exp[j], 0, fchunk(j, fi, used))),
                pl.BlockSpec((1, d, tf), lambda j, fi, exp, used: (exp[j], 0, fchunk(j, fi, used))),
                pl.BlockSpec((1, tf, d), lambda j, fi, exp, used: (exp[j], fchunk(j, fi, used), 0)),
            ],
            out_specs=pl.BlockSpec((tm, d), lambda j, fi, exp, used: (j, 0)),
            scratch_shapes=[pltpu.VMEM((2, n_rows, d), F32), pltpu.VMEM((tm, d), BF16),
                            pltpu.VMEM((tm, d), F32), pltpu.SemaphoreType.DMA((2,))],
        ),
        compiler_params=_cparams(("arbitrary", "arbitrary")),
        name="moe_experts",
    )(tile_expert, tile_used, tok3, tok3, h, wg, wu, wd)


def _combine_kernel(pos_ref, posn_ref, x_ref, gate_ref, g2_ref, fw_ref, ys_ref, out_ref, ybuf, sems, *, rows):
    i = pl.program_id(0)
    slot = i % 2

    def issue(p_ref, sl):
        def body(r, carry):
            for k in range(TOP_K):
                pltpu.make_async_copy(ys_ref.at[pl.ds(p_ref[0, k, r], 1)], ybuf.at[sl, pl.ds(k * rows + r, 1)],
                                      sems.at[sl]).start()
            return carry
        lax.fori_loop(0, rows, body, 0, unroll=DMA_ISSUE_UNROLL)

    @pl.when(i == 0)
    def _():
        issue(pos_ref, 0)

    @pl.when(i + 1 < pl.num_programs(0))
    def _():
        issue(posn_ref, 1 - slot)

    pltpu.make_async_copy(ys_ref.at[pl.ds(0, TOP_K * rows)], ybuf.at[slot], sems.at[slot]).wait()
    g = gate_ref[...]
    y = g[:, 0:1] * ybuf[slot, 0:rows] + g[:, 1:2] * ybuf[slot, rows:2 * rows]
    out_ref[...] = _rms(x_ref[...] + g2_ref[0] * y, fw_ref[...])


def _combine(x, ys, pos, gates, g2, fw, rows):
    n, d = x.shape
    n_steps = n // rows
    steps_per_batch = n_steps // g2.shape[0]
    assert n % rows == 0 and n_steps % g2.shape[0] == 0
    pos3 = jnp.swapaxes(pos.reshape(TOP_K, n_steps, rows), 0, 1)
    smem_tile = functools.partial(pl.BlockSpec, (1, TOP_K, rows), memory_space=pltpu.SMEM)
    return pl.pallas_call(
        functools.partial(_combine_kernel, rows=rows),
        out_shape=jax.ShapeDtypeStruct((n, d), F32),
        grid=(n_steps,),
        in_specs=[
            smem_tile(lambda i: (i, 0, 0)),
            smem_tile(lambda i: (jnp.minimum(i + 1, n_steps - 1), 0, 0)),
            pl.BlockSpec((rows, d), lambda i: (i, 0)),
            pl.BlockSpec((rows, TOP_K), lambda i: (i, 0)),
            pl.BlockSpec((1, 1, d), lambda i: (i // steps_per_batch, 0, 0)),
            pl.BlockSpec((1, d), lambda i: (0, 0)),
            pl.BlockSpec(memory_space=pl.ANY),
        ],
        out_specs=pl.BlockSpec((rows, d), lambda i: (i, 0)),
        scratch_shapes=[pltpu.VMEM((2, TOP_K * rows, d), F32), pltpu.SemaphoreType.DMA((2,))],
        compiler_params=_cparams(("arbitrary",)),
        name="moe_combine",
    )(pos3, pos3, x, gates, g2, fw, ys)


def _pick(n, prefs):
    for p in prefs:
        if n % p == 0:
            return p
    return n


def kernel(x, c, ctx, c_ctx, w_mod, b_mod, norm1_w, norm2_w, final_norm_w, da_w_qkv, da_w_o, da_lambda_q1, da_lambda_k1, da_lambda_q2, da_lambda_k2, da_subln_w, gm_w_in, gm_vnorm_w, gm_vnorm_b, gm_w_s, gm_b_s, gm_w_out, ffn_w_gate, ffn_w_up, ffn_w_down, moe_w_router, moe_w_gate, moe_w_up, moe_w_down):
    b, s_len, d = x.shape
    ctx_len = ctx.shape[1]

    pad = (-(b + 1)) % 8
    cond = jnp.concatenate([c, c_ctx[None, :], jnp.zeros((pad, d), F32)], axis=0)
    mod = _ada(cond, w_mod, b_mod)

    def chunks(layer, r0, r1):
        m = mod[layer, r0:r1]
        return [m[:, None, i * d:(i + 1) * d] for i in range(6)]

    sh1, sc1, g1, sh2, sc2, g2 = chunks(0, 0, b)
    csh1, csc1 = chunks(0, b, b + 1)[:2]
    sh1b, sc1b, g1b, sh2b, sc2b, g2b = chunks(1, 0, b)

    ts = _pick(ctx_len, (256, 128))
    wq = da_w_qkv[0, :, 0:QK_WIDTH]
    wk = da_w_qkv[0, :, QK_WIDTH:2 * QK_WIDTH]
    wv = da_w_qkv[0, :, 2 * QK_WIDTH:]
    k_all, qt, vt5 = _qkv(x, ctx, norm1_w[0][None], sh1, sc1, csh1, csc1,
                          wq.T.astype(BF16), wk.astype(BF16), wv.T.astype(BF16), ts)
    lam_init = 0.8 - 0.6 * math.exp(-0.3 * 0)
    o = _attention(qt, k_all, vt5, da_lambda_q1[0][None], da_lambda_k1[0][None], da_lambda_q2[0][None],
                   da_lambda_k2[0][None], da_subln_w[0][:, None], lam_init, _pick(s_len, (512, 256, 128)))
    x2 = _ffn(x, o, da_w_o[0].astype(BF16), g1, norm2_w[0][None], sh2, sc2, g2,
              ffn_w_gate[0].astype(BF16), ffn_w_up[0].astype(BF16), ffn_w_down[0].astype(BF16),
              _pick(s_len, (512, 256, 128)))

    x3, h3, idx, gates = _gmlp(x2, norm1_w[1][None], sh1b, sc1b, g1b, gm_w_in[0].astype(BF16),
                               gm_vnorm_w[0][None], gm_vnorm_b[0][None], gm_w_s[0].astype(BF16),
                               gm_b_s[0][:, :, None], gm_w_out[0].astype(BF16),
                               norm2_w[1][None], sh2b, sc2b, moe_w_router[0].T, _pick(s_len, (512, 256, 128)))
    n = b * s_len
    ne, _, f = moe_w_gate[0].shape
    tm = _pick(TOP_K * n, (1024, 512, 256, 128))
    pos, slot_token, tile_expert, tile_used = _route(idx, ne, tm)
    ys = _experts(h3.reshape(n, d), slot_token, tile_expert, tile_used, moe_w_gate[0], moe_w_up[0], moe_w_down[0],
                  tm, _pick(f, (512, 256, 128)))
    gates_t = jnp.swapaxes(gates, 1, 2).reshape(n, TOP_K)
    out = _combine(x3.reshape(n, d), ys, pos, gates_t, g2b, final_norm_w[None], _pick(s_len, (256, 128)))
    return out.reshape(b, s_len, d)
```

```python
import functools
import math

import jax
import jax.numpy as jnp
from jax import lax
from jax.experimental import pallas as pl
from jax.experimental.pallas import tpu as pltpu

F32 = jnp.float32
BF16 = jnp.bfloat16
FP8 = jnp.float8_e4m3fn

EPS = 1e-6
GRID_W = 64
ROPE_BASE = 10000.0
N_HEADS = 8
HEAD_DIM = 64
ROPE_FREQS = HEAD_DIM // 4
V_DIM = 2 * HEAD_DIM
QK_WIDTH = N_HEADS * 2 * HEAD_DIM
V_WIDTH = N_HEADS * V_DIM
MAP_FP8_WIDTH = 4 * HEAD_DIM
QK_FP8_WIDTH = 2 * MAP_FP8_WIDTH
K_PRESCALE = 0.25
V_ROWS = V_DIM + 16
KEY_TILES_PER_BLOCK = (5, 4, 3, 2)
SCORE_BUFFERS = 2
DMA_ISSUE_UNROLL = 4
ROW_TILES = (512, 256, 128)
GM_CHUNK = 128
GM_GROUPS = 8
TOP_K = 2
NEG_BIG = -1e30
LOG2_E = math.log2(math.e)

LANES = 128
SUBLANES = 8
VMEM_LIMIT_BYTES = 56 * 1024 * 1024


def _cparams(semantics):
    return pltpu.CompilerParams(dimension_semantics=semantics, vmem_limit_bytes=VMEM_LIMIT_BYTES)


def _rms(x, w):
    return x * lax.rsqrt(jnp.mean(x * x, axis=-1, keepdims=True) + EPS) * w


def _silu(x):
    return x * jax.nn.sigmoid(x)


def _const_spec(shape):
    nd = len(shape)
    return pl.BlockSpec(shape, lambda *_: (0,) * nd, pipeline_mode=pl.Buffered(1))


def _ada_kernel(cond_ref, w_ref, b_ref, o_ref):
    s = _silu(cond_ref[...])
    o_ref[0] = jnp.dot(s, w_ref[0], preferred_element_type=F32, precision=lax.Precision.HIGHEST) + b_ref[0]


def _ada(cond, w_mod, b_mod):
    depth, d, n = w_mod.shape
    rows = cond.shape[0]
    tn = n // 4
    return pl.pallas_call(
        _ada_kernel,
        out_shape=jax.ShapeDtypeStruct((depth, rows, n), F32),
        grid=(depth, n // tn),
        in_specs=[
            pl.BlockSpec((rows, d), lambda l, j: (0, 0)),
            pl.BlockSpec((1, d, tn), lambda l, j: (l, 0, j)),
            pl.BlockSpec((1, 1, tn), lambda l, j: (l, 0, j)),
        ],
        out_specs=pl.BlockSpec((1, rows, tn), lambda l, j: (l, 0, j)),
        compiler_params=_cparams(("arbitrary", "arbitrary")),
        name="ada",
    )(cond, w_mod, b_mod.reshape(depth, 1, n))


def _split_fp8(x):
    hi = x.astype(FP8)
    return hi, (x - hi.astype(F32)).astype(FP8)


def _qkv_kernel(x_ref, ctx_ref, nw_ref, sh_ref, sc_ref, csh_ref, csc_ref, wqt_ref, wk_ref, wvt_ref,
                ck_ref, sk_ref, cq_ref, sq_ref, k_ref, qt_ref, vt_ref, *, n_ctx, ts):
    is_ctx = pl.program_id(1) < n_ctx
    xin = jnp.where(is_ctx, ctx_ref[0], x_ref[0])
    sh = jnp.where(is_ctx, csh_ref[0], sh_ref[0])
    sc = jnp.where(is_ctx, csc_ref[0], sc_ref[0])
    h = (_rms(xin, nw_ref[...]) * (1.0 + sc) + sh).astype(BF16)

    k = jnp.dot(h, wk_ref[...], preferred_element_type=F32)
    lane = lax.broadcasted_iota(jnp.int32, (ts, LANES), 1)
    lower = (lane % 32) < 16
    ck = ck_ref[...]
    sk = sk_ref[...]
    for j in range(N_HEADS):
        kb = k[:, j * LANES:(j + 1) * LANES]
        partner = jnp.where(lower, pltpu.roll(kb, LANES - 16, 1), pltpu.roll(kb, 16, 1))
        hi, lo = _split_fp8((kb * ck + partner * sk) * K_PRESCALE)
        pieces = []
        for c in range(2):
            h_c, l_c = hi[:, c * HEAD_DIM:(c + 1) * HEAD_DIM], lo[:, c * HEAD_DIM:(c + 1) * HEAD_DIM]
            pieces += [h_c, l_c, h_c, l_c]
        k_ref[0, :, j * QK_FP8_WIDTH:(j + 1) * QK_FP8_WIDTH] = jnp.concatenate(pieces, axis=1)

    nt = (((1,), (1,)), ((), ()))
    qt = lax.dot_general(wqt_ref[...], h, nt, preferred_element_type=F32)
    cq = cq_ref[...]
    sq = sq_ref[...]
    scale = HEAD_DIM ** -0.5 * LOG2_E / K_PRESCALE
    for g in range(QK_WIDTH // HEAD_DIM):
        blk = qt[g * HEAD_DIM:(g + 1) * HEAD_DIM]
        partner = jnp.concatenate([blk[16:32], blk[0:16], blk[48:64], blk[32:48]], axis=0)
        hi, lo = _split_fp8((blk * cq + partner * sq) * scale)
        qt_ref[0, 4 * g * HEAD_DIM:4 * (g + 1) * HEAD_DIM, :] = jnp.concatenate([hi, hi, lo, lo], axis=0)

    vt = lax.dot_general(wvt_ref[...], h, nt, preferred_element_type=F32)
    ones_row = (lax.broadcasted_iota(jnp.int32, (V_ROWS - V_DIM, ts), 0) == 0).astype(BF16)
    for hh in range(N_HEADS):
        vt_ref[0, hh, 0, 0:V_DIM, :] = vt[hh * V_DIM:(hh + 1) * V_DIM].astype(BF16)
        vt_ref[0, hh, 0, V_DIM:V_ROWS, :] = ones_row


def _rope_tables(s_len, ctx_len):
    pos = jnp.arange(s_len)
    row = (pos // GRID_W).astype(F32)
    col = (pos % GRID_W).astype(F32)
    inv_freq = ROPE_BASE ** (-jnp.arange(ROPE_FREQS, dtype=F32) / ROPE_FREQS)
    ar = row[:, None] * inv_freq
    ac = col[:, None] * inv_freq
    cos64 = jnp.concatenate([jnp.cos(ar), jnp.cos(ar), jnp.cos(ac), jnp.cos(ac)], axis=-1)
    sin64 = jnp.concatenate([-jnp.sin(ar), jnp.sin(ar), -jnp.sin(ac), jnp.sin(ac)], axis=-1)
    ck = jnp.concatenate([jnp.ones((ctx_len, LANES), F32), jnp.tile(cos64, (1, 2))], axis=0)
    sk = jnp.concatenate([jnp.zeros((ctx_len, LANES), F32), jnp.tile(sin64, (1, 2))], axis=0)
    return ck, sk, cos64.T, sin64.T


def _qkv(x, ctx, nw, sh, sc, csh, csc, wqt, wk, wvt, ts):
    b, s_len, d = x.shape
    ctx_len = ctx.shape[1]
    assert s_len % ts == 0 and ctx_len % ts == 0
    n_ctx = ctx_len // ts
    n_lat = s_len // ts
    ktot = ctx_len + s_len
    sub = _pick(n_ctx + n_lat, KEY_TILES_PER_BLOCK)
    ck, sk, cq, sq = _rope_tables(s_len, ctx_len)

    def lat(t):
        return jnp.maximum(t - n_ctx, 0)

    row = lambda bb, t: (bb, 0, 0)
    return pl.pallas_call(
        functools.partial(_qkv_kernel, n_ctx=n_ctx, ts=ts),
        out_shape=(
            jax.ShapeDtypeStruct((b, ktot, N_HEADS * QK_FP8_WIDTH), FP8),
            jax.ShapeDtypeStruct((b, N_HEADS * QK_FP8_WIDTH, s_len), FP8),
            jax.ShapeDtypeStruct((b, N_HEADS, ktot // (ts * sub), V_ROWS, ts * sub), BF16),
        ),
        grid=(b, n_ctx + n_lat),
        in_specs=[
            pl.BlockSpec((1, ts, d), lambda bb, t: (bb, lat(t), 0)),
            pl.BlockSpec((1, ts, d), lambda bb, t: (bb, jnp.minimum(t, n_ctx - 1), 0)),
            _const_spec((1, d)),
            pl.BlockSpec((1, 1, d), row),
            pl.BlockSpec((1, 1, d), row),
            _const_spec((1, 1, d)),
            _const_spec((1, 1, d)),
            _const_spec((QK_WIDTH, d)),
            _const_spec((d, QK_WIDTH)),
            _const_spec((V_WIDTH, d)),
            pl.BlockSpec((ts, LANES), lambda bb, t: (t, 0)),
            pl.BlockSpec((ts, LANES), lambda bb, t: (t, 0)),
            pl.BlockSpec((HEAD_DIM, ts), lambda bb, t: (0, lat(t))),
            pl.BlockSpec((HEAD_DIM, ts), lambda bb, t: (0, lat(t))),
        ],
        out_specs=(
            pl.BlockSpec((1, ts, N_HEADS * QK_FP8_WIDTH), lambda bb, t: (bb, t, 0)),
            pl.BlockSpec((1, N_HEADS * QK_FP8_WIDTH, ts), lambda bb, t: (bb, 0, lat(t))),
            pl.BlockSpec((1, N_HEADS, 1, V_ROWS, ts), lambda bb, t: (bb, 0, t // sub, 0, t % sub)),
        ),
        compiler_params=_cparams(("arbitrary", "arbitrary")),
        name="qkv",
    )(x, ctx, nw, sh, sc, csh, csc, wqt, wk, wvt, ck, sk, cq, sq)


def _attn_kernel(lq1_ref, lk1_ref, lq2_ref, lk2_ref, subw_ref, q_ref, k_ref, v_ref, o_ref, acc_scr,
                 *bufs, tq, tkb, nb, lam_init):
    nbuf = len(bufs) // 2
    s_bufs, p_bufs = bufs[:nbuf], bufs[nbuf:]
    acc_scr[...] = jnp.zeros_like(acc_scr)

    def score_block(blk, slot):
        off = pl.multiple_of(blk * tkb, tkb)
        block_max = []
        for c in range(2):
            feats = slice(c * MAP_FP8_WIDTH, (c + 1) * MAP_FP8_WIDTH)
            s = jnp.dot(k_ref[0, pl.ds(off, tkb), feats], q_ref[0, feats, :], preferred_element_type=F32)
            s_bufs[slot][:, c * tq:(c + 1) * tq] = s
            block_max.append(jnp.max(s, axis=0, keepdims=True))
        return jnp.concatenate(block_max, axis=1)

    def softmax(slot, m, block_max):
        m_new = jnp.maximum(m, block_max)
        p_bufs[slot][...] = jnp.exp2(s_bufs[slot][...] - m_new).astype(BF16)
        return m_new, jnp.exp2(m - m_new)

    def values(j, slot, alpha):
        pv = jnp.dot(v_ref[0, 0, j], p_bufs[slot][...], preferred_element_type=F32)
        acc_scr[...] = acc_scr[...] * alpha + pv

    def step(j, slot, m, alpha, block_max):
        next_max = score_block(j + 2, (slot + 2) % nbuf)
        m, alpha_next = softmax((slot + 1) % nbuf, m, block_max)
        values(j, slot, alpha)
        return m, alpha_next, next_max

    m, alpha = softmax(0, jnp.full((1, 2 * tq), NEG_BIG, F32), score_block(0, 0))
    block_max = score_block(1, 1) if nb > 1 else None
    n_full = max(nb - 2, 0)

    def unrolled(i, carry):
        for r in range(nbuf):
            carry = step(nbuf * i + r, r, *carry)
        return carry

    if n_full:
        m, alpha, block_max = lax.fori_loop(0, n_full // nbuf, unrolled, (m, alpha, block_max))
    for j in range(nbuf * (n_full // nbuf), n_full):
        m, alpha, block_max = step(j, j % nbuf, m, alpha, block_max)
    if nb > 1:
        m, alpha_next = softmax((nb - 1) % nbuf, m, block_max)
        values(nb - 2, (nb - 2) % nbuf, alpha)
        alpha = alpha_next
    values(nb - 1, (nb - 1) % nbuf, alpha)

    lam = (jnp.exp(jnp.sum(lq1_ref[...] * lk1_ref[...], axis=-1, keepdims=True))
           - jnp.exp(jnp.sum(lq2_ref[...] * lk2_ref[...], axis=-1, keepdims=True)) + lam_init)
    acc = acc_scr[0:V_DIM, :]
    inv = 1.0 / acc_scr[V_DIM:V_DIM + 1, :]
    o = acc[:, 0:tq] * inv[:, 0:tq] - lam * (acc[:, tq:2 * tq] * inv[:, tq:2 * tq])
    y = o * lax.rsqrt(jnp.mean(o * o, axis=0, keepdims=True) + EPS) * subw_ref[...]
    y = y * (1.0 - lam_init)
    o_ref[0] = y.T.astype(BF16)


def _attention(qt, k_all, vt5, lq1, lk1, lq2, lk2, subw, lam_init, tq):
    b, _, s_len = qt.shape
    ktot = k_all.shape[1]
    nb, tkb = vt5.shape[2], vt5.shape[4]
    assert s_len % tq == 0 and nb * tkb == ktot and vt5.shape[3] == V_ROWS
    vec = _const_spec((1, HEAD_DIM))
    return pl.pallas_call(
        functools.partial(_attn_kernel, tq=tq, tkb=tkb, nb=nb, lam_init=lam_init),
        out_shape=jax.ShapeDtypeStruct((b, s_len, V_WIDTH), BF16),
        grid=(b, N_HEADS, s_len // tq),
        in_specs=[
            vec, vec, vec, vec,
            _const_spec((V_DIM, 1)),
            pl.BlockSpec((1, QK_FP8_WIDTH, tq), lambda bb, hh, i: (bb, hh, i)),
            pl.BlockSpec((1, ktot, QK_FP8_WIDTH), lambda bb, hh, i: (bb, 0, hh), pipeline_mode=pl.Buffered(1)),
            pl.BlockSpec((1, 1, nb, V_ROWS, tkb), lambda bb, hh, i: (bb, hh, 0, 0, 0),
                         pipeline_mode=pl.Buffered(1)),
        ],
        out_specs=pl.BlockSpec((1, tq, V_DIM), lambda bb, hh, i: (bb, i, hh)),
        scratch_shapes=[
            pltpu.VMEM((V_ROWS, 2 * tq), F32),
        ] + [pltpu.VMEM((tkb, 2 * tq), F32)] * SCORE_BUFFERS + [pltpu.VMEM((tkb, 2 * tq), BF16)] * SCORE_BUFFERS,
        compiler_params=_cparams(("arbitrary", "arbitrary", "arbitrary")),
        name="attention",
    )(lq1, lk1, lq2, lk2, subw, qt, k_all, vt5)


def _ffn_kernel(x_ref, o_ref, wo_ref, g1_ref, nw_ref, sh_ref, sc_ref, g2_ref, wg_ref, wu_ref, wd_ref, out_ref):
    y = jnp.dot(o_ref[0], wo_ref[...], preferred_element_type=F32)
    x1 = x_ref[0] + g1_ref[0] * y
    h = (_rms(x1, nw_ref[...]) * (1.0 + sc_ref[0]) + sh_ref[0]).astype(BF16)
    a = jnp.dot(h, wg_ref[...], preferred_element_type=F32)
    u = jnp.dot(h, wu_ref[...], preferred_element_type=F32)
    z = (_silu(a) * u).astype(BF16)
    out_ref[0] = x1 + g2_ref[0] * jnp.dot(z, wd_ref[...], preferred_element_type=F32)


def _ffn(x, o, wo, g1, nw, sh, sc, g2, wg, wu, wd, ts):
    b, s_len, d = x.shape
    f = wg.shape[1]
    row = lambda bb, t: (bb, 0, 0)
    tile = lambda bb, t: (bb, t, 0)
    vec = pl.BlockSpec((1, 1, d), row)
    return pl.pallas_call(
        _ffn_kernel,
        out_shape=jax.ShapeDtypeStruct((b, s_len, d), F32),
        grid=(b, s_len // ts),
        in_specs=[
            pl.BlockSpec((1, ts, d), tile),
            pl.BlockSpec((1, ts, V_WIDTH), tile),
            _const_spec((V_WIDTH, d)),
            vec, _const_spec((1, d)), vec, vec, vec,
            _const_spec((d, f)), _const_spec((d, f)), _const_spec((f, d)),
        ],
        out_specs=pl.BlockSpec((1, ts, d), tile),
        compiler_params=_cparams(("arbitrary", "arbitrary")),
        name="ffn",
    )(x, o, wo, g1, nw, sh, sc, g2, wg, wu, wd)


def _gmlp_kernel(x_ref, nw1_ref, sh1_ref, sc1_ref, g1_ref, win_ref, vnw_ref, vnb_ref, ws_ref, bs_ref, wout_ref,
                 nw2_ref, sh2_ref, sc2_ref, wrt_ref, x3_ref, h3_ref, idx_ref, gate_ref, t_scr, *, ts, half):
    x2 = x_ref[0]
    h = (_rms(x2, nw1_ref[...]) * (1.0 + sc1_ref[0]) + sh1_ref[0]).astype(BF16)
    z = jnp.dot(h, win_ref[...], preferred_element_type=F32)
    z = 0.5 * z * (1.0 + lax.erf(z * (2.0 ** -0.5)))
    u = z[:, 0:half]
    v = z[:, half:2 * half]
    mu = jnp.mean(v, axis=-1, keepdims=True)
    vc = v - mu
    v = vc * lax.rsqrt(jnp.mean(vc * vc, axis=-1, keepdims=True) + EPS) * vnw_ref[...] + vnb_ref[...]
    vb = v.astype(BF16)
    gd = half // GM_GROUPS
    for n in range(ts // GM_CHUNK):
        rows = slice(n * GM_CHUNK, (n + 1) * GM_CHUNK)
        for g in range(GM_GROUPS):
            cols = slice(g * gd, (g + 1) * gd)
            s = jnp.dot(ws_ref[g], vb[rows, cols], preferred_element_type=F32) + bs_ref[g]
            t_scr[rows, cols] = (u[rows, cols] * s).astype(BF16)
    x3 = x2 + g1_ref[0] * jnp.dot(t_scr[...], wout_ref[...], preferred_element_type=F32)
    x3_ref[0] = x3

    h3 = _rms(x3, nw2_ref[...]) * (1.0 + sc2_ref[0]) + sh2_ref[0]
    h3_ref[0] = h3
    nt = (((1,), (1,)), ((), ()))
    logits = lax.dot_general(wrt_ref[...], h3, nt, preferred_element_type=F32,
                             precision=lax.Precision.HIGHEST)
    ne = logits.shape[0]
    eid = lax.broadcasted_iota(jnp.int32, logits.shape, 0)
    m1 = jnp.max(logits, axis=0, keepdims=True)
    i1 = jnp.min(jnp.where(logits == m1, eid, ne), axis=0, keepdims=True)
    rest = jnp.where(eid == i1, -jnp.inf, logits)
    m2 = jnp.max(rest, axis=0, keepdims=True)
    i2 = jnp.min(jnp.where(rest == m2, eid, ne), axis=0, keepdims=True)
    e2 = jnp.exp(m2 - m1)
    den = 1.0 + e2
    idx_ref[0] = jnp.concatenate([i1, i2], axis=0)
    gate_ref[0] = jnp.concatenate([1.0 / den, e2 / den], axis=0)


def _gmlp(x, nw1, sh1, sc1, g1, win, vnw, vnb, ws, bs, wout, nw2, sh2, sc2, wrt, ts):
    b, s_len, d = x.shape
    half = wout.shape[0]
    ne = wrt.shape[0]
    assert ts % GM_CHUNK == 0 and s_len % ts == 0
    row = lambda bb, t: (bb, 0, 0)
    tile = lambda bb, t: (bb, t, 0)
    vec = pl.BlockSpec((1, 1, d), row)
    return pl.pallas_call(
        functools.partial(_gmlp_kernel, ts=ts, half=half),
        out_shape=(
            jax.ShapeDtypeStruct((b, s_len, d), F32),
            jax.ShapeDtypeStruct((b, s_len, d), F32),
            jax.ShapeDtypeStruct((b, TOP_K, s_len), jnp.int32),
            jax.ShapeDtypeStruct((b, TOP_K, s_len), F32),
        ),
        grid=(b, s_len // ts),
        in_specs=[
            pl.BlockSpec((1, ts, d), tile),
            _const_spec((1, d)), vec, vec, vec,
            _const_spec((d, 2 * half)),
            _const_spec((1, half)), _const_spec((1, half)),
            _const_spec((GM_GROUPS, GM_CHUNK, GM_CHUNK)),
            _const_spec((GM_GROUPS, GM_CHUNK, 1)),
            _const_spec((half, d)),
            _const_spec((1, d)), vec, vec,
            _const_spec((ne, d)),
        ],
        out_specs=(
            pl.BlockSpec((1, ts, d), tile),
            pl.BlockSpec((1, ts, d), tile),
            pl.BlockSpec((1, TOP_K, ts), lambda bb, t: (bb, 0, t)),
            pl.BlockSpec((1, TOP_K, ts), lambda bb, t: (bb, 0, t)),
        ),
        scratch_shapes=[pltpu.VMEM((ts, half), BF16)],
        compiler_params=_cparams(("arbitrary", "arbitrary")),
        name="gmlp",
    )(x, nw1, sh1, sc1, g1, win, vnw, vnb, ws, bs, wout, nw2, sh2, sc2, wrt)


def _route(idx, n_experts, tm):
    b, k, s_len = idx.shape
    n = b * s_len
    e_flat = jnp.swapaxes(idx, 0, 1).reshape(k * n)
    onehot = (e_flat[:, None] == jnp.arange(n_experts, dtype=jnp.int32)[None, :]).astype(jnp.int32)
    csum = jnp.cumsum(onehot, axis=0)
    counts = csum[-1]
    rank = jnp.take_along_axis(csum, e_flat[:, None], axis=1)[:, 0] - 1
    padded = ((counts + tm - 1) // tm) * tm
    ends = jnp.cumsum(padded)
    starts = ends - padded
    pos = (starts[e_flat] + rank).astype(jnp.int32)
    n_tiles = (k * n) // tm + n_experts + 1
    tile_ids = jnp.arange(n_tiles, dtype=jnp.int32)
    used = tile_ids * tm < ends[-1]
    tile_src = jnp.where(used, tile_ids, ends[-1] // tm - 1).astype(jnp.int32)
    tile_expert = jnp.minimum(jnp.searchsorted(ends, tile_src * tm, side="right"), n_experts - 1).astype(jnp.int32)
    tokens = jnp.tile(jnp.arange(n, dtype=jnp.int32), k)
    slot_token = jnp.zeros((n_tiles * tm,), jnp.int32).at[pos].set(tokens, unique_indices=True)
    return pos.reshape(k, n), slot_token, tile_expert, used.astype(jnp.int32)


def _experts_kernel(exp_ref, used_ref, tok_ref, tokn_ref, h_ref, wg_ref, wu_ref, wd_ref, y_ref,
                    xbuf, xb_scr, acc_scr, sems, *, tm, per_step):
    j = pl.program_id(0)
    f = pl.program_id(1)
    n_rows = per_step * pl.num_programs(1)
    used = used_ref[j] > 0
    slot = j % 2

    def row_copy(t_ref, r, sl):
        return pltpu.make_async_copy(h_ref.at[pl.ds(t_ref[0, 0, r], 1)], xbuf.at[sl, pl.ds(r, 1)], sems.at[sl])

    @pl.when(f == 0)
    def _():
        @pl.when(j == 0)
        def _():
            def body(r, carry):
                row_copy(tok_ref, r, 0).start()
                return carry
            lax.fori_loop(0, n_rows, body, 0, unroll=DMA_ISSUE_UNROLL)

        @pl.when(used_ref[jnp.maximum(j - 1, 0)] > 0)
        def _():
            pltpu.make_async_copy(h_ref.at[pl.ds(0, n_rows)], xbuf.at[slot, pl.ds(0, n_rows)], sems.at[slot]).wait()
            xb_scr[...] = xbuf[slot, 0:tm].astype(BF16)

        acc_scr[...] = jnp.zeros_like(acc_scr)

    @pl.when(used)
    def _():
        for i in range(per_step):
            row_copy(tokn_ref, f * per_step + i, 1 - slot).start()
        xb = xb_scr[...]
        a = jnp.dot(xb, wg_ref[0].astype(BF16), preferred_element_type=F32)
        u = jnp.dot(xb, wu_ref[0].astype(BF16), preferred_element_type=F32)
        z = (_silu(a) * u).astype(BF16)
        acc_scr[...] += jnp.dot(z, wd_ref[0].astype(BF16), preferred_element_type=F32)

    @pl.when(f == pl.num_programs(1) - 1)
    def _():
        y_ref[...] = acc_scr[...]


def _experts(h, slot_token, tile_expert, tile_used, wg, wu, wd, tm, tf):
    n, d = h.shape
    ne, _, f = wg.shape
    nf = f // tf
    n_tiles = tile_expert.shape[0]
    per_step = -(-tm // (nf * SUBLANES)) * SUBLANES
    n_rows = per_step * nf
    assert slot_token.shape[0] == n_tiles * tm and f % tf == 0 and n_rows <= n
    tok3 = jnp.pad(slot_token.reshape(n_tiles, 1, tm), ((0, 0), (0, 0), (0, n_rows - tm)))

    def fchunk(j, fi, used):
        return jnp.where(used[j] > 0, fi, nf - 1)

    smem_tile = functools.partial(pl.BlockSpec, (1, 1, n_rows), memory_space=pltpu.SMEM)
    return pl.pallas_call(
        functools.partial(_experts_kernel, tm=tm, per_step=per_step),
        out_shape=jax.ShapeDtypeStruct((n_tiles * tm, d), F32),
        grid_spec=pltpu.PrefetchScalarGridSpec(
            num_scalar_prefetch=2,
            grid=(n_tiles, nf),
            in_specs=[
                smem_tile(lambda j, fi, exp, used: (j, 0, 0)),
                smem_tile(lambda j, fi, exp, used: (jnp.minimum(j + 1, n_tiles - 1), 0, 0)),
                pl.BlockSpec(memory_space=pl.ANY),
                pl.BlockSpec((1, d, tf), lambda j, fi, exp, used: (exp[j], 0, fchunk(j, fi, used))),
                pl.BlockSpec((1, d, tf), lambda j, fi, exp, used: (exp[j], 0, fchunk(j, fi, used))),
                pl.BlockSpec((1, tf, d), lambda j, fi, exp, used: (exp[j], fchunk(j, fi, used), 0)),
            ],
            out_specs=pl.BlockSpec((tm, d), lambda j, fi, exp, used: (j, 0)),
            scratch_shapes=[pltpu.VMEM((2, n_rows, d), F32), pltpu.VMEM((tm, d), BF16),
                            pltpu.VMEM((tm, d), F32), pltpu.SemaphoreType.DMA((2,))],
        ),
        compiler_params=_cparams(("arbitrary", "arbitrary")),
        name="moe_experts",
    )(tile_expert, tile_used, tok3, tok3, h, wg, wu, wd)


def _combine_kernel(pos_ref, posn_ref, x_ref, gate_ref, g2_ref, fw_ref, ys_ref, out_ref, ybuf, sems, *, rows):
    i = pl.program_id(0)
    slot = i % 2

    def issue(p_ref, sl):
        def body(r, carry):
            for k in range(TOP_K):
                pltpu.make_async_copy(ys_ref.at[pl.ds(p_ref[0, k, r], 1)], ybuf.at[sl, pl.ds(k * rows + r, 1)],
                                      sems.at[sl]).start()
            return carry
        lax.fori_loop(0, rows, body, 0, unroll=DMA_ISSUE_UNROLL)

    @pl.when(i == 0)
    def _():
        issue(pos_ref, 0)

    @pl.when(i + 1 < pl.num_programs(0))
    def _():
        issue(posn_ref, 1 - slot)

    pltpu.make_async_copy(ys_ref.at[pl.ds(0, TOP_K * rows)], ybuf.at[slot], sems.at[slot]).wait()
    g = gate_ref[...]
    y = g[:, 0:1] * ybuf[slot, 0:rows] + g[:, 1:2] * ybuf[slot, rows:2 * rows]
    out_ref[...] = _rms(x_ref[...] + g2_ref[0] * y, fw_ref[...])


def _combine(x, ys, pos, gates, g2, fw, rows):
    n, d = x.shape
    n_steps = n // rows
    steps_per_batch = n_steps // g2.shape[0]
    assert n % rows == 0 and n_steps % g2.shape[0] == 0
    pos3 = jnp.swapaxes(pos.reshape(TOP_K, n_steps, rows), 0, 1)
    smem_tile = functools.partial(pl.BlockSpec, (1, TOP_K, rows), memory_space=pltpu.SMEM)
    return pl.pallas_call(
        functools.partial(_combine_kernel, rows=rows),
        out_shape=jax.ShapeDtypeStruct((n, d), F32),
        grid=(n_steps,),
        in_specs=[
            smem_tile(lambda i: (i, 0, 0)),
            smem_tile(lambda i: (jnp.minimum(i + 1, n_steps - 1), 0, 0)),
            pl.BlockSpec((rows, d), lambda i: (i, 0)),
            pl.BlockSpec((rows, TOP_K), lambda i: (i, 0)),
            pl.BlockSpec((1, 1, d), lambda i: (i // steps_per_batch, 0, 0)),
            pl.BlockSpec((1, d), lambda i: (0, 0)),
            pl.BlockSpec(memory_space=pl.ANY),
        ],
        out_specs=pl.BlockSpec((rows, d), lambda i: (i, 0)),
        scratch_shapes=[pltpu.VMEM((2, TOP_K * rows, d), F32), pltpu.SemaphoreType.DMA((2,))],
        compiler_params=_cparams(("arbitrary",)),
        name="moe_combine",
    )(pos3, pos3, x, gates, g2, fw, ys)


def _pick(n, prefs):
    for p in prefs:
        if n % p == 0:
            return p
    return n


def kernel(x, c, ctx, c_ctx, w_mod, b_mod, norm1_w, norm2_w, final_norm_w, da_w_qkv, da_w_o, da_lambda_q1, da_lambda_k1, da_lambda_q2, da_lambda_k2, da_subln_w, gm_w_in, gm_vnorm_w, gm_vnorm_b, gm_w_s, gm_b_s, gm_w_out, ffn_w_gate, ffn_w_up, ffn_w_down, moe_w_router, moe_w_gate, moe_w_up, moe_w_down):
    b, s_len, d = x.shape
    ctx_len = ctx.shape[1]

    pad = (-(b + 1)) % SUBLANES
    cond = jnp.concatenate([c, c_ctx[None, :], jnp.zeros((pad, d), F32)], axis=0)
    mod = _ada(cond, w_mod, b_mod)

    def chunks(layer, r0, r1):
        m = mod[layer, r0:r1]
        return [m[:, None, i * d:(i + 1) * d] for i in range(6)]

    sh1, sc1, g1, sh2, sc2, g2 = chunks(0, 0, b)
    csh1, csc1 = chunks(0, b, b + 1)[:2]
    sh1b, sc1b, g1b, sh2b, sc2b, g2b = chunks(1, 0, b)

    ts = _pick(ctx_len, ROW_TILES[1:])
    wq = da_w_qkv[0, :, 0:QK_WIDTH]
    wk = da_w_qkv[0, :, QK_WIDTH:2 * QK_WIDTH]
    wv = da_w_qkv[0, :, 2 * QK_WIDTH:]
    k_all, qt, vt5 = _qkv(x, ctx, norm1_w[0][None], sh1, sc1, csh1, csc1,
                          wq.T.astype(BF16), wk.astype(BF16), wv.T.astype(BF16), ts)
    lam_init = 0.8 - 0.6 * math.exp(-0.3 * 0)
    o = _attention(qt, k_all, vt5, da_lambda_q1[0][None], da_lambda_k1[0][None], da_lambda_q2[0][None],
                   da_lambda_k2[0][None], da_subln_w[0][:, None], lam_init, _pick(s_len, (1024,) + ROW_TILES))
    x2 = _ffn(x, o, da_w_o[0].astype(BF16), g1, norm2_w[0][None], sh2, sc2, g2,
              ffn_w_gate[0].astype(BF16), ffn_w_up[0].astype(BF16), ffn_w_down[0].astype(BF16),
              _pick(s_len, ROW_TILES))

    x3, h3, idx, gates = _gmlp(x2, norm1_w[1][None], sh1b, sc1b, g1b, gm_w_in[0].astype(BF16),
                               gm_vnorm_w[0][None], gm_vnorm_b[0][None], gm_w_s[0].astype(BF16),
                               gm_b_s[0][:, :, None], gm_w_out[0].astype(BF16),
                               norm2_w[1][None], sh2b, sc2b, moe_w_router[0].T, _pick(s_len, ROW_TILES))
    n = b * s_len
    ne, _, f = moe_w_gate[0].shape
    tm = _pick(TOP_K * n, (1024,) + ROW_TILES)
    pos, slot_token, tile_expert, tile_used = _route(idx, ne, tm)
    ys = _experts(h3.reshape(n, d), slot_token, tile_expert, tile_used, moe_w_gate[0], moe_w_up[0], moe_w_down[0],
                  tm, _pick(f, ROW_TILES))
    gates_t = jnp.swapaxes(gates, 1, 2).reshape(n, TOP_K)
    out = _combine(x3.reshape(n, d), ys, pos, gates_t, g2b, final_norm_w[None], _pick(s_len, ROW_TILES[1:]))
    return out.reshape(b, s_len, d)
```

```python
import functools
import math

import jax
import jax.numpy as jnp
from jax import lax
from jax.experimental import pallas as pl
from jax.experimental.pallas import tpu as pltpu

F32 = jnp.float32
BF16 = jnp.bfloat16
FP8 = jnp.float8_e4m3fn

EPS = 1e-6
GRID_W = 64
ROPE_BASE = 10000.0
N_HEADS = 8
HEAD_DIM = 64
ROPE_FREQS = HEAD_DIM // 4
V_DIM = 2 * HEAD_DIM
QK_WIDTH = N_HEADS * 2 * HEAD_DIM
V_WIDTH = N_HEADS * V_DIM
MAP_FP8_WIDTH = 4 * HEAD_DIM
QK_FP8_WIDTH = 2 * MAP_FP8_WIDTH
K_PRESCALE = 0.25
V_ROWS = V_DIM + 16
KEY_TILES_PER_BLOCK = (5, 4, 3, 2)
SCORE_BUFFERS = 2
DMA_ISSUE_UNROLL = 4
ROW_TILES = (512, 256, 128)
GM_CHUNK = 128
GM_GROUPS = 8
TOP_K = 2
NEG_BIG = -1e30
LOG2_E = math.log2(math.e)

LANES = 128
SUBLANES = 8
VMEM_LIMIT_BYTES = 56 * 1024 * 1024


def _cparams(semantics):
    return pltpu.CompilerParams(dimension_semantics=semantics, vmem_limit_bytes=VMEM_LIMIT_BYTES)


def _rms(x, w):
    return x * lax.rsqrt(jnp.mean(x * x, axis=-1, keepdims=True) + EPS) * w


def _silu(x):
    return x * jax.nn.sigmoid(x)


def _const_spec(shape):
    nd = len(shape)
    return pl.BlockSpec(shape, lambda *_: (0,) * nd, pipeline_mode=pl.Buffered(1))


def _ada_kernel(cond_ref, w_ref, b_ref, o_ref):
    s = _silu(cond_ref[...])
    o_ref[0] = jnp.dot(s, w_ref[0], preferred_element_type=F32, precision=lax.Precision.HIGHEST) + b_ref[0]


def _ada(cond, w_mod, b_mod):
    depth, d, n = w_mod.shape
    rows = cond.shape[0]
    tn = n // 4
    return pl.pallas_call(
        _ada_kernel,
        out_shape=jax.ShapeDtypeStruct((depth, rows, n), F32),
        grid=(depth, n // tn),
        in_specs=[
            pl.BlockSpec((rows, d), lambda l, j: (0, 0)),
            pl.BlockSpec((1, d, tn), lambda l, j: (l, 0, j)),
            pl.BlockSpec((1, 1, tn), lambda l, j: (l, 0, j)),
        ],
        out_specs=pl.BlockSpec((1, rows, tn), lambda l, j: (l, 0, j)),
        compiler_params=_cparams(("arbitrary", "arbitrary")),
        name="ada",
    )(cond, w_mod, b_mod.reshape(depth, 1, n))


def _split_fp8(x):
    hi = x.astype(FP8)
    return hi, (x - hi.astype(F32)).astype(FP8)


def _qkv_kernel(x_ref, ctx_ref, nw_ref, sh_ref, sc_ref, csh_ref, csc_ref, wqt_ref, wk_ref, wvt_ref,
                ck_ref, sk_ref, cq_ref, sq_ref, k_ref, qt_ref, vt_ref, *, n_ctx, ts):
    is_ctx = pl.program_id(1) < n_ctx
    xin = jnp.where(is_ctx, ctx_ref[0], x_ref[0])
    sh = jnp.where(is_ctx, csh_ref[0], sh_ref[0])
    sc = jnp.where(is_ctx, csc_ref[0], sc_ref[0])
    h = (_rms(xin, nw_ref[...]) * (1.0 + sc) + sh).astype(BF16)

    k = jnp.dot(h, wk_ref[...], preferred_element_type=F32)
    lane = lax.broadcasted_iota(jnp.int32, (ts, LANES), 1)
    lower = (lane % 32) < 16
    ck = ck_ref[...]
    sk = sk_ref[...]
    for j in range(N_HEADS):
        kb = k[:, j * LANES:(j + 1) * LANES]
        partner = jnp.where(lower, pltpu.roll(kb, LANES - 16, 1), pltpu.roll(kb, 16, 1))
        hi, lo = _split_fp8((kb * ck + partner * sk) * K_PRESCALE)
        pieces = []
        for c in range(2):
            h_c, l_c = hi[:, c * HEAD_DIM:(c + 1) * HEAD_DIM], lo[:, c * HEAD_DIM:(c + 1) * HEAD_DIM]
            pieces += [h_c, l_c, h_c, l_c]
        k_ref[0, :, j * QK_FP8_WIDTH:(j + 1) * QK_FP8_WIDTH] = jnp.concatenate(pieces, axis=1)

    nt = (((1,), (1,)), ((), ()))
    qt = lax.dot_general(wqt_ref[...], h, nt, preferred_element_type=F32)
    cq = cq_ref[...]
    sq = sq_ref[...]
    scale = HEAD_DIM ** -0.5 * LOG2_E / K_PRESCALE
    for g in range(QK_WIDTH // HEAD_DIM):
        blk = qt[g * HEAD_DIM:(g + 1) * HEAD_DIM]
        partner = jnp.concatenate([blk[16:32], blk[0:16], blk[48:64], blk[32:48]], axis=0)
        hi, lo = _split_fp8((blk * cq + partner * sq) * scale)
        qt_ref[0, 4 * g * HEAD_DIM:4 * (g + 1) * HEAD_DIM, :] = jnp.concatenate([hi, hi, lo, lo], axis=0)

    vt = lax.dot_general(wvt_ref[...], h, nt, preferred_element_type=F32)
    ones_row = (lax.broadcasted_iota(jnp.int32, (V_ROWS - V_DIM, ts), 0) == 0).astype(BF16)
    for hh in range(N_HEADS):
        vt_ref[0, hh, 0, 0:V_DIM, :] = vt[hh * V_DIM:(hh + 1) * V_DIM].astype(BF16)
        vt_ref[0, hh, 0, V_DIM:V_ROWS, :] = ones_row


def _rope_tables(s_len, ctx_len):
    pos = jnp.arange(s_len)
    row = (pos // GRID_W).astype(F32)
    col = (pos % GRID_W).astype(F32)
    inv_freq = ROPE_BASE ** (-jnp.arange(ROPE_FREQS, dtype=F32) / ROPE_FREQS)
    ar = row[:, None] * inv_freq
    ac = col[:, None] * inv_freq
    cos64 = jnp.concatenate([jnp.cos(ar), jnp.cos(ar), jnp.cos(ac), jnp.cos(ac)], axis=-1)
    sin64 = jnp.concatenate([-jnp.sin(ar), jnp.sin(ar), -jnp.sin(ac), jnp.sin(ac)], axis=-1)
    ck = jnp.concatenate([jnp.ones((ctx_len, LANES), F32), jnp.tile(cos64, (1, 2))], axis=0)
    sk = jnp.concatenate([jnp.zeros((ctx_len, LANES), F32), jnp.tile(sin64, (1, 2))], axis=0)
    return ck, sk, cos64.T, sin64.T


def _qkv(x, ctx, nw, sh, sc, csh, csc, wqt, wk, wvt, ts):
    b, s_len, d = x.shape
    ctx_len = ctx.shape[1]
    assert s_len % ts == 0 and ctx_len % ts == 0
    n_ctx = ctx_len // ts
    n_lat = s_len // ts
    ktot = ctx_len + s_len
    sub = _pick(n_ctx + n_lat, KEY_TILES_PER_BLOCK)
    ck, sk, cq, sq = _rope_tables(s_len, ctx_len)

    def lat(t):
        return jnp.maximum(t - n_ctx, 0)

    row = lambda bb, t: (bb, 0, 0)
    return pl.pallas_call(
        functools.partial(_qkv_kernel, n_ctx=n_ctx, ts=ts),
        out_shape=(
            jax.ShapeDtypeStruct((b, ktot, N_HEADS * QK_FP8_WIDTH), FP8),
            jax.ShapeDtypeStruct((b, N_HEADS * QK_FP8_WIDTH, s_len), FP8),
            jax.ShapeDtypeStruct((b, N_HEADS, ktot // (ts * sub), V_ROWS, ts * sub), BF16),
        ),
        grid=(b, n_ctx + n_lat),
        in_specs=[
            pl.BlockSpec((1, ts, d), lambda bb, t: (bb, lat(t), 0)),
            pl.BlockSpec((1, ts, d), lambda bb, t: (bb, jnp.minimum(t, n_ctx - 1), 0)),
            _const_spec((1, d)),
            pl.BlockSpec((1, 1, d), row),
            pl.BlockSpec((1, 1, d), row),
            _const_spec((1, 1, d)),
            _const_spec((1, 1, d)),
            _const_spec((QK_WIDTH, d)),
            _const_spec((d, QK_WIDTH)),
            _const_spec((V_WIDTH, d)),
            pl.BlockSpec((ts, LANES), lambda bb, t: (t, 0)),
            pl.BlockSpec((ts, LANES), lambda bb, t: (t, 0)),
            pl.BlockSpec((HEAD_DIM, ts), lambda bb, t: (0, lat(t))),
            pl.BlockSpec((HEAD_DIM, ts), lambda bb, t: (0, lat(t))),
        ],
        out_specs=(
            pl.BlockSpec((1, ts, N_HEADS * QK_FP8_WIDTH), lambda bb, t: (bb, t, 0)),
            pl.BlockSpec((1, N_HEADS * QK_FP8_WIDTH, ts), lambda bb, t: (bb, 0, lat(t))),
            pl.BlockSpec((1, N_HEADS, 1, V_ROWS, ts), lambda bb, t: (bb, 0, t // sub, 0, t % sub)),
        ),
        compiler_params=_cparams(("arbitrary", "arbitrary")),
        name="qkv",
    )(x, ctx, nw, sh, sc, csh, csc, wqt, wk, wvt, ck, sk, cq, sq)


def _attn_kernel(lq1_ref, lk1_ref, lq2_ref, lk2_ref, subw_ref, q_ref, k_ref, v_ref, o_ref, acc_scr,
                 *bufs, tq, tkb, nb, lam_init):
    nbuf = len(bufs) // 2
    s_bufs, p_bufs = bufs[:nbuf], bufs[nbuf:]
    acc_scr[...] = jnp.zeros_like(acc_scr)

    def score_block(blk, slot):
        off = pl.multiple_of(blk * tkb, tkb)
        block_max = []
        for c in range(2):
            feats = slice(c * MAP_FP8_WIDTH, (c + 1) * MAP_FP8_WIDTH)
            s = jnp.dot(k_ref[0, pl.ds(off, tkb), feats], q_ref[0, feats, :], preferred_element_type=F32)
            s_bufs[slot][:, c * tq:(c + 1) * tq] = s
            block_max.append(jnp.max(s, axis=0, keepdims=True))
        return jnp.concatenate(block_max, axis=1)

    def softmax(slot, m, block_max):
        m_new = jnp.maximum(m, block_max)
        p_bufs[slot][...] = jnp.exp2(s_bufs[slot][...] - m_new).astype(BF16)
        return m_new, jnp.exp2(m - m_new)

    def values(j, slot, alpha):
        pv = jnp.dot(v_ref[0, 0, j], p_bufs[slot][...], preferred_element_type=F32)
        acc_scr[...] = acc_scr[...] * alpha + pv

    def step(j, slot, m, alpha, block_max):
        next_max = score_block(j + 2, (slot + 2) % nbuf)
        m, alpha_next = softmax((slot + 1) % nbuf, m, block_max)
        values(j, slot, alpha)
        return m, alpha_next, next_max

    m, alpha = softmax(0, jnp.full((1, 2 * tq), NEG_BIG, F32), score_block(0, 0))
    block_max = score_block(1, 1) if nb > 1 else None
    n_full = max(nb - 2, 0)

    def unrolled(i, carry):
        for r in range(nbuf):
            carry = step(nbuf * i + r, r, *carry)
        return carry

    if n_full:
        m, alpha, block_max = lax.fori_loop(0, n_full // nbuf, unrolled, (m, alpha, block_max))
    for j in range(nbuf * (n_full // nbuf), n_full):
        m, alpha, block_max = step(j, j % nbuf, m, alpha, block_max)
    if nb > 1:
        m, alpha_next = softmax((nb - 1) % nbuf, m, block_max)
        values(nb - 2, (nb - 2) % nbuf, alpha)
        alpha = alpha_next
    values(nb - 1, (nb - 1) % nbuf, alpha)

    lam = (jnp.exp(jnp.sum(lq1_ref[...] * lk1_ref[...], axis=-1, keepdims=True))
           - jnp.exp(jnp.sum(lq2_ref[...] * lk2_ref[...], axis=-1, keepdims=True)) + lam_init)
    acc = acc_scr[0:V_DIM, :]
    inv = 1.0 / acc_scr[V_DIM:V_DIM + 1, :]
    o = acc[:, 0:tq] * inv[:, 0:tq] - lam * (acc[:, tq:2 * tq] * inv[:, tq:2 * tq])
    y = o * lax.rsqrt(jnp.mean(o * o, axis=0, keepdims=True) + EPS) * subw_ref[...]
    y = y * (1.0 - lam_init)
    o_ref[0] = y.T.astype(BF16)


def _attention(qt, k_all, vt5, lq1, lk1, lq2, lk2, subw, lam_init, tq):
    b, _, s_len = qt.shape
    ktot = k_all.shape[1]
    nb, tkb = vt5.shape[2], vt5.shape[4]
    assert s_len % tq == 0 and nb * tkb == ktot and vt5.shape[3] == V_ROWS
    vec = _const_spec((1, HEAD_DIM))
    return pl.pallas_call(
        functools.partial(_attn_kernel, tq=tq, tkb=tkb, nb=nb, lam_init=lam_init),
        out_shape=jax.ShapeDtypeStruct((b, s_len, V_WIDTH), BF16),
        grid=(b, N_HEADS, s_len // tq),
        in_specs=[
            vec, vec, vec, vec,
            _const_spec((V_DIM, 1)),
            pl.BlockSpec((1, QK_FP8_WIDTH, tq), lambda bb, hh, i: (bb, hh, i)),
            pl.BlockSpec((1, ktot, QK_FP8_WIDTH), lambda bb, hh, i: (bb, 0, hh), pipeline_mode=pl.Buffered(1)),
            pl.BlockSpec((1, 1, nb, V_ROWS, tkb), lambda bb, hh, i: (bb, hh, 0, 0, 0),
                         pipeline_mode=pl.Buffered(1)),
        ],
        out_specs=pl.BlockSpec((1, tq, V_DIM), lambda bb, hh, i: (bb, i, hh)),
        scratch_shapes=[
            pltpu.VMEM((V_ROWS, 2 * tq), F32),
        ] + [pltpu.VMEM((tkb, 2 * tq), F32)] * SCORE_BUFFERS + [pltpu.VMEM((tkb, 2 * tq), BF16)] * SCORE_BUFFERS,
        compiler_params=_cparams(("arbitrary", "arbitrary", "arbitrary")),
        name="attention",
    )(lq1, lk1, lq2, lk2, subw, qt, k_all, vt5)


def _ffn_kernel(x_ref, o_ref, wo_ref, g1_ref, nw_ref, sh_ref, sc_ref, g2_ref, wg_ref, wu_ref, wd_ref, out_ref):
    y = jnp.dot(o_ref[0], wo_ref[...], preferred_element_type=F32)
    x1 = x_ref[0] + g1_ref[0] * y
    h = (_rms(x1, nw_ref[...]) * (1.0 + sc_ref[0]) + sh_ref[0]).astype(BF16)
    a = jnp.dot(h, wg_ref[...], preferred_element_type=F32)
    u = jnp.dot(h, wu_ref[...], preferred_element_type=F32)
    z = (_silu(a) * u).astype(BF16)
    out_ref[0] = x1 + g2_ref[0] * jnp.dot(z, wd_ref[...], preferred_element_type=F32)


def _ffn(x, o, wo, g1, nw, sh, sc, g2, wg, wu, wd, ts):
    b, s_len, d = x.shape
    f = wg.shape[1]
    row = lambda bb, t: (bb, 0, 0)
    tile = lambda bb, t: (bb, t, 0)
    vec = pl.BlockSpec((1, 1, d), row)
    return pl.pallas_call(
        _ffn_kernel,
        out_shape=jax.ShapeDtypeStruct((b, s_len, d), F32),
        grid=(b, s_len // ts),
        in_specs=[
            pl.BlockSpec((1, ts, d), tile),
            pl.BlockSpec((1, ts, V_WIDTH), tile),
            _const_spec((V_WIDTH, d)),
            vec, _const_spec((1, d)), vec, vec, vec,
            _const_spec((d, f)), _const_spec((d, f)), _const_spec((f, d)),
        ],
        out_specs=pl.BlockSpec((1, ts, d), tile),
        compiler_params=_cparams(("arbitrary", "arbitrary")),
        name="ffn",
    )(x, o, wo, g1, nw, sh, sc, g2, wg, wu, wd)


def _gmlp_kernel(x_ref, nw1_ref, sh1_ref, sc1_ref, g1_ref, win_ref, vnw_ref, vnb_ref, ws_ref, bs_ref, wout_ref,
                 nw2_ref, sh2_ref, sc2_ref, wrt_ref, x3_ref, h3_ref, idx_ref, gate_ref, t_scr, *, ts, half):
    x2 = x_ref[0]
    h = (_rms(x2, nw1_ref[...]) * (1.0 + sc1_ref[0]) + sh1_ref[0]).astype(BF16)
    z = jnp.dot(h, win_ref[...], preferred_element_type=F32)
    z = 0.5 * z * (1.0 + lax.erf(z * (2.0 ** -0.5)))
    u = z[:, 0:half]
    v = z[:, half:2 * half]
    mu = jnp.mean(v, axis=-1, keepdims=True)
    vc = v - mu
    v = vc * lax.rsqrt(jnp.mean(vc * vc, axis=-1, keepdims=True) + EPS) * vnw_ref[...] + vnb_ref[...]
    vb = v.astype(BF16)
    gd = half // GM_GROUPS
    for n in range(ts // GM_CHUNK):
        rows = slice(n * GM_CHUNK, (n + 1) * GM_CHUNK)
        for g in range(GM_GROUPS):
            cols = slice(g * gd, (g + 1) * gd)
            s = jnp.dot(ws_ref[g], vb[rows, cols], preferred_element_type=F32) + bs_ref[g]
            t_scr[rows, cols] = (u[rows, cols] * s).astype(BF16)
    x3 = x2 + g1_ref[0] * jnp.dot(t_scr[...], wout_ref[...], preferred_element_type=F32)
    x3_ref[0] = x3

    h3 = _rms(x3, nw2_ref[...]) * (1.0 + sc2_ref[0]) + sh2_ref[0]
    h3_ref[0] = h3
    nt = (((1,), (1,)), ((), ()))
    logits = lax.dot_general(wrt_ref[...], h3, nt, preferred_element_type=F32,
                             precision=lax.Precision.HIGHEST)
    ne = logits.shape[0]
    eid = lax.broadcasted_iota(jnp.int32, logits.shape, 0)
    m1 = jnp.max(logits, axis=0, keepdims=True)
    i1 = jnp.min(jnp.where(logits == m1, eid, ne), axis=0, keepdims=True)
    rest = jnp.where(eid == i1, -jnp.inf, logits)
    m2 = jnp.max(rest, axis=0, keepdims=True)
    i2 = jnp.min(jnp.where(rest == m2, eid, ne), axis=0, keepdims=True)
    e2 = jnp.exp(m2 - m1)
    den = 1.0 + e2
    idx_ref[0] = jnp.concatenate([i1, i2], axis=0)
    gate_ref[0] = jnp.concatenate([1.0 / den, e2 / den], axis=0)


def _gmlp(x, nw1, sh1, sc1, g1, win, vnw, vnb, ws, bs, wout, nw2, sh2, sc2, wrt, ts):
    b, s_len, d = x.shape
    half = wout.shape[0]
    ne = wrt.shape[0]
    assert ts % GM_CHUNK == 0 and s_len % ts == 0
    row = lambda bb, t: (bb, 0, 0)
    tile = lambda bb, t: (bb, t, 0)
    vec = pl.BlockSpec((1, 1, d), row)
    return pl.pallas_call(
        functools.partial(_gmlp_kernel, ts=ts, half=half),
        out_shape=(
            jax.ShapeDtypeStruct((b, s_len, d), F32),
            jax.ShapeDtypeStruct((b, s_len, d), F32),
            jax.ShapeDtypeStruct((b, TOP_K, s_len), jnp.int32),
            jax.ShapeDtypeStruct((b, TOP_K, s_len), F32),
        ),
        grid=(b, s_len // ts),
        in_specs=[
            pl.BlockSpec((1, ts, d), tile),
            _const_spec((1, d)), vec, vec, vec,
            _const_spec((d, 2 * half)),
            _const_spec((1, half)), _const_spec((1, half)),
            _const_spec((GM_GROUPS, GM_CHUNK, GM_CHUNK)),
            _const_spec((GM_GROUPS, GM_CHUNK, 1)),
            _const_spec((half, d)),
            _const_spec((1, d)), vec, vec,
            _const_spec((ne, d)),
        ],
        out_specs=(
            pl.BlockSpec((1, ts, d), tile),
            pl.BlockSpec((1, ts, d), tile),
            pl.BlockSpec((1, TOP_K, ts), lambda bb, t: (bb, 0, t)),
            pl.BlockSpec((1, TOP_K, ts), lambda bb, t: (bb, 0, t)),
        ),
        scratch_shapes=[pltpu.VMEM((ts, half), BF16)],
        compiler_params=_cparams(("arbitrary", "arbitrary")),
        name="gmlp",
    )(x, nw1, sh1, sc1, g1, win, vnw, vnb, ws, bs, wout, nw2, sh2, sc2, wrt)


def _route(idx, n_experts, tm):
    b, k, s_len = idx.shape
    n = b * s_len
    e_flat = jnp.swapaxes(idx, 0, 1).reshape(k * n)
    onehot = (e_flat[:, None] == jnp.arange(n_experts, dtype=jnp.int32)[None, :]).astype(jnp.int32)
    csum = jnp.cumsum(onehot, axis=0)
    counts = csum[-1]
    rank = jnp.take_along_axis(csum, e_flat[:, None], axis=1)[:, 0] - 1
    padded = ((counts + tm - 1) // tm) * tm
    ends = jnp.cumsum(padded)
    starts = ends - padded
    pos = (starts[e_flat] + rank).astype(jnp.int32)
    n_tiles = (k * n) // tm + n_experts + 1
    tile_ids = jnp.arange(n_tiles, dtype=jnp.int32)
    used = tile_ids * tm < ends[-1]
    tile_src = jnp.where(used, tile_ids, ends[-1] // tm - 1).astype(jnp.int32)
    tile_expert = jnp.minimum(jnp.searchsorted(ends, tile_src * tm, side="right"), n_experts - 1).astype(jnp.int32)
    tokens = jnp.tile(jnp.arange(n, dtype=jnp.int32), k)
    slot_token = jnp.zeros((n_tiles * tm,), jnp.int32).at[pos].set(tokens, unique_indices=True)
    return pos.reshape(k, n), slot_token, tile_expert, used.astype(jnp.int32)


def _experts_kernel(exp_ref, used_ref, tok_ref, tokn_ref, h_ref, wg_ref, wu_ref, wd_ref, y_ref,
                    xbuf, xb_scr, acc_scr, sems, *, tm, per_step):
    j = pl.program_id(0)
    f = pl.program_id(1)
    n_rows = per_step * pl.num_programs(1)
    used = used_ref[j] > 0
    slot = j % 2

    def row_copy(t_ref, r, sl):
        return pltpu.make_async_copy(h_ref.at[pl.ds(t_ref[0, 0, r], 1)], xbuf.at[sl, pl.ds(r, 1)], sems.at[sl])

    @pl.when(f == 0)
    def _():
        @pl.when(j == 0)
        def _():
            def body(r, carry):
                row_copy(tok_ref, r, 0).start()
                return carry
            lax.fori_loop(0, n_rows, body, 0, unroll=DMA_ISSUE_UNROLL)

        @pl.when(used_ref[jnp.maximum(j - 1, 0)] > 0)
        def _():
            pltpu.make_async_copy(h_ref.at[pl.ds(0, n_rows)], xbuf.at[slot, pl.ds(0, n_rows)], sems.at[slot]).wait()
            xb_scr[...] = xbuf[slot, 0:tm].astype(BF16)

        acc_scr[...] = jnp.zeros_like(acc_scr)

    @pl.when(used)
    def _():
        for i in range(per_step):
            row_copy(tokn_ref, f * per_step + i, 1 - slot).start()
        xb = xb_scr[...]
        a = jnp.dot(xb, wg_ref[0], preferred_element_type=F32)
        u = jnp.dot(xb, wu_ref[0], preferred_element_type=F32)
        z = (_silu(a) * u).astype(BF16)
        acc_scr[...] += jnp.dot(z, wd_ref[0], preferred_element_type=F32)

    @pl.when(f == pl.num_programs(1) - 1)
    def _():
        y_ref[...] = acc_scr[...]


def _experts(h, slot_token, tile_expert, tile_used, wg, wu, wd, tm, tf):
    n, d = h.shape
    ne, _, f = wg.shape
    nf = f // tf
    n_tiles = tile_expert.shape[0]
    per_step = -(-tm // (nf * SUBLANES)) * SUBLANES
    n_rows = per_step * nf
    assert slot_token.shape[0] == n_tiles * tm and f % tf == 0 and n_rows <= n
    tok3 = jnp.pad(slot_token.reshape(n_tiles, 1, tm), ((0, 0), (0, 0), (0, n_rows - tm)))

    def fchunk(j, fi, used):
        return jnp.where(used[j] > 0, fi, nf - 1)

    smem_tile = functools.partial(pl.BlockSpec, (1, 1, n_rows), memory_space=pltpu.SMEM)
    return pl.pallas_call(
        functools.partial(_experts_kernel, tm=tm, per_step=per_step),
        out_shape=jax.ShapeDtypeStruct((n_tiles * tm, d), F32),
        grid_spec=pltpu.PrefetchScalarGridSpec(
            num_scalar_prefetch=2,
            grid=(n_tiles, nf),
            in_specs=[
                smem_tile(lambda j, fi, exp, used: (j, 0, 0)),
                smem_tile(lambda j, fi, exp, used: (jnp.minimum(j + 1, n_tiles - 1), 0, 0)),
                pl.BlockSpec(memory_space=pl.ANY),
                pl.BlockSpec((1, d, tf), lambda j, fi, exp, used: (exp[j], 0, fchunk(j, fi, used))),
                pl.BlockSpec((1, d, tf), lambda j, fi, exp, used: (exp[j], 0, fchunk(j, fi, used))),
                pl.BlockSpec((1, tf, d), lambda j, fi, exp, used: (exp[j], fchunk(j, fi, used), 0)),
            ],
            out_specs=pl.BlockSpec((tm, d), lambda j, fi, exp, used: (j, 0)),
            scratch_shapes=[pltpu.VMEM((2, n_rows, d), F32), pltpu.VMEM((tm, d), BF16),
                            pltpu.VMEM((tm, d), F32), pltpu.SemaphoreType.DMA((2,))],
        ),
        compiler_params=_cparams(("arbitrary", "arbitrary")),
        name="moe_experts",
    )(tile_expert, tile_used, tok3, tok3, h, wg, wu, wd)


def _combine_kernel(pos_ref, posn_ref, x_ref, gate_ref, g2_ref, fw_ref, ys_ref, out_ref, ybuf, sems, *, rows):
    i = pl.program_id(0)
    slot = i % 2

    def issue(p_ref, sl):
        def body(r, carry):
            for k in range(TOP_K):
                pltpu.make_async_copy(ys_ref.at[pl.ds(p_ref[0, k, r], 1)], ybuf.at[sl, pl.ds(k * rows + r, 1)],
                                      sems.at[sl]).start()
            return carry
        lax.fori_loop(0, rows, body, 0, unroll=DMA_ISSUE_UNROLL)

    @pl.when(i == 0)
    def _():
        issue(pos_ref, 0)

    @pl.when(i + 1 < pl.num_programs(0))
    def _():
        issue(posn_ref, 1 - slot)

    pltpu.make_async_copy(ys_ref.at[pl.ds(0, TOP_K * rows)], ybuf.at[slot], sems.at[slot]).wait()
    g = gate_ref[...]
    y = g[:, 0:1] * ybuf[slot, 0:rows] + g[:, 1:2] * ybuf[slot, rows:2 * rows]
    out_ref[...] = _rms(x_ref[...] + g2_ref[0] * y, fw_ref[...])


def _combine(x, ys, pos, gates, g2, fw, rows):
    n, d = x.shape
    n_steps = n // rows
    steps_per_batch = n_steps // g2.shape[0]
    assert n % rows == 0 and n_steps % g2.shape[0] == 0
    pos3 = jnp.swapaxes(pos.reshape(TOP_K, n_steps, rows), 0, 1)
    smem_tile = functools.partial(pl.BlockSpec, (1, TOP_K, rows), memory_space=pltpu.SMEM)
    return pl.pallas_call(
        functools.partial(_combine_kernel, rows=rows),
        out_shape=jax.ShapeDtypeStruct((n, d), F32),
        grid=(n_steps,),
        in_specs=[
            smem_tile(lambda i: (i, 0, 0)),
            smem_tile(lambda i: (jnp.minimum(i + 1, n_steps - 1), 0, 0)),
            pl.BlockSpec((rows, d), lambda i: (i, 0)),
            pl.BlockSpec((rows, TOP_K), lambda i: (i, 0)),
            pl.BlockSpec((1, 1, d), lambda i: (i // steps_per_batch, 0, 0)),
            pl.BlockSpec((1, d), lambda i: (0, 0)),
            pl.BlockSpec(memory_space=pl.ANY),
        ],
        out_specs=pl.BlockSpec((rows, d), lambda i: (i, 0)),
        scratch_shapes=[pltpu.VMEM((2, TOP_K * rows, d), F32), pltpu.SemaphoreType.DMA((2,))],
        compiler_params=_cparams(("arbitrary",)),
        name="moe_combine",
    )(pos3, pos3, x, gates, g2, fw, ys)


def _pick(n, prefs):
    for p in prefs:
        if n % p == 0:
            return p
    return n


def kernel(x, c, ctx, c_ctx, w_mod, b_mod, norm1_w, norm2_w, final_norm_w, da_w_qkv, da_w_o, da_lambda_q1, da_lambda_k1, da_lambda_q2, da_lambda_k2, da_subln_w, gm_w_in, gm_vnorm_w, gm_vnorm_b, gm_w_s, gm_b_s, gm_w_out, ffn_w_gate, ffn_w_up, ffn_w_down, moe_w_router, moe_w_gate, moe_w_up, moe_w_down):
    b, s_len, d = x.shape
    ctx_len = ctx.shape[1]

    pad = (-(b + 1)) % SUBLANES
    cond = jnp.concatenate([c, c_ctx[None, :], jnp.zeros((pad, d), F32)], axis=0)
    mod = _ada(cond, w_mod, b_mod)

    def chunks(layer, r0, r1):
        m = mod[layer, r0:r1]
        return [m[:, None, i * d:(i + 1) * d] for i in range(6)]

    sh1, sc1, g1, sh2, sc2, g2 = chunks(0, 0, b)
    csh1, csc1 = chunks(0, b, b + 1)[:2]
    sh1b, sc1b, g1b, sh2b, sc2b, g2b = chunks(1, 0, b)

    ts = _pick(ctx_len, ROW_TILES[1:])
    wq = da_w_qkv[0, :, 0:QK_WIDTH]
    wk = da_w_qkv[0, :, QK_WIDTH:2 * QK_WIDTH]
    wv = da_w_qkv[0, :, 2 * QK_WIDTH:]
    k_all, qt, vt5 = _qkv(x, ctx, norm1_w[0][None], sh1, sc1, csh1, csc1,
                          wq.T.astype(BF16), wk.astype(BF16), wv.T.astype(BF16), ts)
    lam_init = 0.8 - 0.6 * math.exp(-0.3 * 0)
    o = _attention(qt, k_all, vt5, da_lambda_q1[0][None], da_lambda_k1[0][None], da_lambda_q2[0][None],
                   da_lambda_k2[0][None], da_subln_w[0][:, None], lam_init, _pick(s_len, (1024,) + ROW_TILES))
    x2 = _ffn(x, o, da_w_o[0].astype(BF16), g1, norm2_w[0][None], sh2, sc2, g2,
              ffn_w_gate[0].astype(BF16), ffn_w_up[0].astype(BF16), ffn_w_down[0].astype(BF16),
              _pick(s_len, ROW_TILES))

    x3, h3, idx, gates = _gmlp(x2, norm1_w[1][None], sh1b, sc1b, g1b, gm_w_in[0].astype(BF16),
                               gm_vnorm_w[0][None], gm_vnorm_b[0][None], gm_w_s[0].astype(BF16),
                               gm_b_s[0][:, :, None], gm_w_out[0].astype(BF16),
                               norm2_w[1][None], sh2b, sc2b, moe_w_router[0].T, _pick(s_len, ROW_TILES))
    n = b * s_len
    ne, _, f = moe_w_gate[0].shape
    tm = _pick(TOP_K * n, (1024,) + ROW_TILES)
    pos, slot_token, tile_expert, tile_used = _route(idx, ne, tm)
    ys = _experts(h3.reshape(n, d), slot_token, tile_expert, tile_used, moe_w_gate[0].astype(BF16),
                  moe_w_up[0].astype(BF16), moe_w_down[0].astype(BF16), tm, _pick(f, ROW_TILES))
    gates_t = jnp.swapaxes(gates, 1, 2).reshape(n, TOP_K)
    out = _combine(x3.reshape(n, d), ys, pos, gates_t, g2b, final_norm_w[None], _pick(s_len, ROW_TILES[1:]))
    return out.reshape(b, s_len, d)
```

```python
import functools
import math

import jax
import jax.numpy as jnp
from jax import lax
from jax.experimental import pallas as pl
from jax.experimental.pallas import tpu as pltpu

F32 = jnp.float32
BF16 = jnp.bfloat16
FP8 = jnp.float8_e4m3fn

EPS = 1e-6
GRID_W = 64
ROPE_BASE = 10000.0
N_HEADS = 8
HEAD_DIM = 64
ROPE_FREQS = HEAD_DIM // 4
V_DIM = 2 * HEAD_DIM
QK_WIDTH = N_HEADS * 2 * HEAD_DIM
V_WIDTH = N_HEADS * V_DIM
MAP_FP8_WIDTH = 4 * HEAD_DIM
QK_FP8_WIDTH = 2 * MAP_FP8_WIDTH
K_PRESCALE = 0.25
V_ROWS = V_DIM + 16
KEY_TILES_PER_BLOCK = (5, 4, 3, 2)
SCORE_BUFFERS = 2
DMA_ISSUE_UNROLL = 4
ROW_TILES = (512, 256, 128)
GM_CHUNK = 128
GM_GROUPS = 8
TOP_K = 2
NEG_BIG = -1e30
LOG2_E = math.log2(math.e)

LANES = 128
SUBLANES = 8
VMEM_LIMIT_BYTES = 56 * 1024 * 1024


def _cparams(semantics):
    return pltpu.CompilerParams(dimension_semantics=semantics, vmem_limit_bytes=VMEM_LIMIT_BYTES)


def _rms(x, w):
    return x * lax.rsqrt(jnp.mean(x * x, axis=-1, keepdims=True) + EPS) * w


def _silu(x):
    return x * jax.nn.sigmoid(x)


def _const_spec(shape):
    nd = len(shape)
    return pl.BlockSpec(shape, lambda *_: (0,) * nd, pipeline_mode=pl.Buffered(1))


def _ada_kernel(cond_ref, w_ref, b_ref, o_ref):
    s = _silu(cond_ref[...])
    o_ref[0] = jnp.dot(s, w_ref[0], preferred_element_type=F32, precision=lax.Precision.HIGHEST) + b_ref[0]


def _ada(cond, w_mod, b_mod):
    depth, d, n = w_mod.shape
    rows = cond.shape[0]
    tn = n // 4
    return pl.pallas_call(
        _ada_kernel,
        out_shape=jax.ShapeDtypeStruct((depth, rows, n), F32),
        grid=(depth, n // tn),
        in_specs=[
            pl.BlockSpec((rows, d), lambda l, j: (0, 0)),
            pl.BlockSpec((1, d, tn), lambda l, j: (l, 0, j)),
            pl.BlockSpec((1, 1, tn), lambda l, j: (l, 0, j)),
        ],
        out_specs=pl.BlockSpec((1, rows, tn), lambda l, j: (l, 0, j)),
        compiler_params=_cparams(("arbitrary", "arbitrary")),
        name="ada",
    )(cond, w_mod, b_mod.reshape(depth, 1, n))


def _split_fp8(x):
    hi = x.astype(FP8)
    return hi, (x - hi.astype(F32)).astype(FP8)


def _qkv_kernel(x_ref, ctx_ref, nw_ref, sh_ref, sc_ref, csh_ref, csc_ref, wqt_ref, wk_ref, wvt_ref,
                ck_ref, sk_ref, cq_ref, sq_ref, k_ref, qt_ref, vt_ref, *, n_ctx, ts):
    is_ctx = pl.program_id(1) < n_ctx
    xin = jnp.where(is_ctx, ctx_ref[0], x_ref[0])
    sh = jnp.where(is_ctx, csh_ref[0], sh_ref[0])
    sc = jnp.where(is_ctx, csc_ref[0], sc_ref[0])
    h = (_rms(xin, nw_ref[...]) * (1.0 + sc) + sh).astype(BF16)

    k = jnp.dot(h, wk_ref[...], preferred_element_type=F32)
    lane = lax.broadcasted_iota(jnp.int32, (ts, LANES), 1)
    lower = (lane % 32) < 16
    ck = ck_ref[...]
    sk = sk_ref[...]
    for j in range(N_HEADS):
        kb = k[:, j * LANES:(j + 1) * LANES]
        partner = jnp.where(lower, pltpu.roll(kb, LANES - 16, 1), pltpu.roll(kb, 16, 1))
        hi, lo = _split_fp8((kb * ck + partner * sk) * K_PRESCALE)
        pieces = []
        for c in range(2):
            h_c, l_c = hi[:, c * HEAD_DIM:(c + 1) * HEAD_DIM], lo[:, c * HEAD_DIM:(c + 1) * HEAD_DIM]
            pieces += [h_c, l_c, h_c, l_c]
        k_ref[0, :, j * QK_FP8_WIDTH:(j + 1) * QK_FP8_WIDTH] = jnp.concatenate(pieces, axis=1)

    nt = (((1,), (1,)), ((), ()))
    qt = lax.dot_general(wqt_ref[...], h, nt, preferred_element_type=F32)
    cq = cq_ref[...]
    sq = sq_ref[...]
    scale = HEAD_DIM ** -0.5 * LOG2_E / K_PRESCALE
    for g in range(QK_WIDTH // HEAD_DIM):
        blk = qt[g * HEAD_DIM:(g + 1) * HEAD_DIM]
        partner = jnp.concatenate([blk[16:32], blk[0:16], blk[48:64], blk[32:48]], axis=0)
        hi, lo = _split_fp8((blk * cq + partner * sq) * scale)
        qt_ref[0, 4 * g * HEAD_DIM:4 * (g + 1) * HEAD_DIM, :] = jnp.concatenate([hi, hi, lo, lo], axis=0)

    vt = lax.dot_general(wvt_ref[...], h, nt, preferred_element_type=F32)
    ones_row = (lax.broadcasted_iota(jnp.int32, (V_ROWS - V_DIM, ts), 0) == 0).astype(BF16)
    for hh in range(N_HEADS):
        vt_ref[0, hh, 0, 0:V_DIM, :] = vt[hh * V_DIM:(hh + 1) * V_DIM].astype(BF16)
        vt_ref[0, hh, 0, V_DIM:V_ROWS, :] = ones_row


def _rope_tables(s_len, ctx_len):
    pos = jnp.arange(s_len)
    row = (pos // GRID_W).astype(F32)
    col = (pos % GRID_W).astype(F32)
    inv_freq = ROPE_BASE ** (-jnp.arange(ROPE_FREQS, dtype=F32) / ROPE_FREQS)
    ar = row[:, None] * inv_freq
    ac = col[:, None] * inv_freq
    cos64 = jnp.concatenate([jnp.cos(ar), jnp.cos(ar), jnp.cos(ac), jnp.cos(ac)], axis=-1)
    sin64 = jnp.concatenate([-jnp.sin(ar), jnp.sin(ar), -jnp.sin(ac), jnp.sin(ac)], axis=-1)
    ck = jnp.concatenate([jnp.ones((ctx_len, LANES), F32), jnp.tile(cos64, (1, 2))], axis=0)
    sk = jnp.concatenate([jnp.zeros((ctx_len, LANES), F32), jnp.tile(sin64, (1, 2))], axis=0)
    return ck, sk, cos64.T, sin64.T


def _qkv(x, ctx, nw, sh, sc, csh, csc, wqt, wk, wvt, ts):
    b, s_len, d = x.shape
    ctx_len = ctx.shape[1]
    assert s_len % ts == 0 and ctx_len % ts == 0
    n_ctx = ctx_len // ts
    n_lat = s_len // ts
    ktot = ctx_len + s_len
    sub = _pick(n_ctx + n_lat, KEY_TILES_PER_BLOCK)
    ck, sk, cq, sq = _rope_tables(s_len, ctx_len)

    def lat(t):
        return jnp.maximum(t - n_ctx, 0)

    row = lambda bb, t: (bb, 0, 0)
    return pl.pallas_call(
        functools.partial(_qkv_kernel, n_ctx=n_ctx, ts=ts),
        out_shape=(
            jax.ShapeDtypeStruct((b, ktot, N_HEADS * QK_FP8_WIDTH), FP8),
            jax.ShapeDtypeStruct((b, N_HEADS * QK_FP8_WIDTH, s_len), FP8),
            jax.ShapeDtypeStruct((b, N_HEADS, ktot // (ts * sub), V_ROWS, ts * sub), BF16),
        ),
        grid=(b, n_ctx + n_lat),
        in_specs=[
            pl.BlockSpec((1, ts, d), lambda bb, t: (bb, lat(t), 0)),
            pl.BlockSpec((1, ts, d), lambda bb, t: (bb, jnp.minimum(t, n_ctx - 1), 0)),
            _const_spec((1, d)),
            pl.BlockSpec((1, 1, d), row),
            pl.BlockSpec((1, 1, d), row),
            _const_spec((1, 1, d)),
            _const_spec((1, 1, d)),
            _const_spec((QK_WIDTH, d)),
            _const_spec((d, QK_WIDTH)),
            _const_spec((V_WIDTH, d)),
            pl.BlockSpec((ts, LANES), lambda bb, t: (t, 0)),
            pl.BlockSpec((ts, LANES), lambda bb, t: (t, 0)),
            pl.BlockSpec((HEAD_DIM, ts), lambda bb, t: (0, lat(t))),
            pl.BlockSpec((HEAD_DIM, ts), lambda bb, t: (0, lat(t))),
        ],
        out_specs=(
            pl.BlockSpec((1, ts, N_HEADS * QK_FP8_WIDTH), lambda bb, t: (bb, t, 0)),
            pl.BlockSpec((1, N_HEADS * QK_FP8_WIDTH, ts), lambda bb, t: (bb, 0, lat(t))),
            pl.BlockSpec((1, N_HEADS, 1, V_ROWS, ts), lambda bb, t: (bb, 0, t // sub, 0, t % sub)),
        ),
        compiler_params=_cparams(("arbitrary", "arbitrary")),
        name="qkv",
    )(x, ctx, nw, sh, sc, csh, csc, wqt, wk, wvt, ck, sk, cq, sq)


def _attn_kernel(lq1_ref, lk1_ref, lq2_ref, lk2_ref, subw_ref, q_ref, k_ref, v_ref, o_ref, acc_scr,
                 *bufs, tq, tkb, nb, lam_init):
    nbuf = len(bufs) // 2
    s_bufs, p_bufs = bufs[:nbuf], bufs[nbuf:]
    acc_scr[...] = jnp.zeros_like(acc_scr)

    def score_block(blk, slot):
        off = pl.multiple_of(blk * tkb, tkb)
        block_max = []
        for c in range(2):
            feats = slice(c * MAP_FP8_WIDTH, (c + 1) * MAP_FP8_WIDTH)
            s = jnp.dot(k_ref[0, pl.ds(off, tkb), feats], q_ref[0, feats, :], preferred_element_type=F32)
            s_bufs[slot][:, c * tq:(c + 1) * tq] = s
            block_max.append(jnp.max(s, axis=0, keepdims=True))
        return jnp.concatenate(block_max, axis=1)

    def softmax(slot, m, block_max):
        m_new = jnp.maximum(m, block_max)
        p_bufs[slot][...] = jnp.exp2(s_bufs[slot][...] - m_new).astype(BF16)
        return m_new, jnp.exp2(m - m_new)

    def values(j, slot, alpha):
        pv = jnp.dot(v_ref[0, 0, j], p_bufs[slot][...], preferred_element_type=F32)
        acc_scr[...] = acc_scr[...] * alpha + pv

    def step(j, slot, m, alpha, block_max):
        next_max = score_block(j + 2, (slot + 2) % nbuf)
        m, alpha_next = softmax((slot + 1) % nbuf, m, block_max)
        values(j, slot, alpha)
        return m, alpha_next, next_max

    m, alpha = softmax(0, jnp.full((1, 2 * tq), NEG_BIG, F32), score_block(0, 0))
    block_max = score_block(1, 1) if nb > 1 else None
    n_full = max(nb - 2, 0)

    def unrolled(i, carry):
        for r in range(nbuf):
            carry = step(nbuf * i + r, r, *carry)
        return carry

    if n_full:
        m, alpha, block_max = lax.fori_loop(0, n_full // nbuf, unrolled, (m, alpha, block_max))
    for j in range(nbuf * (n_full // nbuf), n_full):
        m, alpha, block_max = step(j, j % nbuf, m, alpha, block_max)
    if nb > 1:
        m, alpha_next = softmax((nb - 1) % nbuf, m, block_max)
        values(nb - 2, (nb - 2) % nbuf, alpha)
        alpha = alpha_next
    values(nb - 1, (nb - 1) % nbuf, alpha)

    lam = (jnp.exp(jnp.sum(lq1_ref[...] * lk1_ref[...], axis=-1, keepdims=True))
           - jnp.exp(jnp.sum(lq2_ref[...] * lk2_ref[...], axis=-1, keepdims=True)) + lam_init)
    acc = acc_scr[0:V_DIM, :]
    inv = 1.0 / acc_scr[V_DIM:V_DIM + 1, :]
    o = acc[:, 0:tq] * inv[:, 0:tq] - lam * (acc[:, tq:2 * tq] * inv[:, tq:2 * tq])
    y = o * lax.rsqrt(jnp.mean(o * o, axis=0, keepdims=True) + EPS) * subw_ref[...]
    y = y * (1.0 - lam_init)
    o_ref[0] = y.T.astype(BF16)


def _attention(qt, k_all, vt5, lq1, lk1, lq2, lk2, subw, lam_init, tq):
    b, _, s_len = qt.shape
    ktot = k_all.shape[1]
    nb, tkb = vt5.shape[2], vt5.shape[4]
    assert s_len % tq == 0 and nb * tkb == ktot and vt5.shape[3] == V_ROWS
    vec = _const_spec((1, HEAD_DIM))
    return pl.pallas_call(
        functools.partial(_attn_kernel, tq=tq, tkb=tkb, nb=nb, lam_init=lam_init),
        out_shape=jax.ShapeDtypeStruct((b, s_len, V_WIDTH), BF16),
        grid=(b, N_HEADS, s_len // tq),
        in_specs=[
            vec, vec, vec, vec,
            _const_spec((V_DIM, 1)),
            pl.BlockSpec((1, QK_FP8_WIDTH, tq), lambda bb, hh, i: (bb, hh, i)),
            pl.BlockSpec((1, ktot, QK_FP8_WIDTH), lambda bb, hh, i: (bb, 0, hh), pipeline_mode=pl.Buffered(1)),
            pl.BlockSpec((1, 1, nb, V_ROWS, tkb), lambda bb, hh, i: (bb, hh, 0, 0, 0),
                         pipeline_mode=pl.Buffered(1)),
        ],
        out_specs=pl.BlockSpec((1, tq, V_DIM), lambda bb, hh, i: (bb, i, hh)),
        scratch_shapes=[
            pltpu.VMEM((V_ROWS, 2 * tq), F32),
        ] + [pltpu.VMEM((tkb, 2 * tq), F32)] * SCORE_BUFFERS + [pltpu.VMEM((tkb, 2 * tq), BF16)] * SCORE_BUFFERS,
        compiler_params=_cparams(("arbitrary", "arbitrary", "arbitrary")),
        name="attention",
    )(lq1, lk1, lq2, lk2, subw, qt, k_all, vt5)


def _ffn_kernel(x_ref, o_ref, wo_ref, g1_ref, nw_ref, sh_ref, sc_ref, g2_ref, wg_ref, wu_ref, wd_ref, out_ref):
    y = jnp.dot(o_ref[0], wo_ref[...], preferred_element_type=F32)
    x1 = x_ref[0] + g1_ref[0] * y
    h = (_rms(x1, nw_ref[...]) * (1.0 + sc_ref[0]) + sh_ref[0]).astype(BF16)
    a = jnp.dot(h, wg_ref[...], preferred_element_type=F32)
    u = jnp.dot(h, wu_ref[...], preferred_element_type=F32)
    z = (_silu(a) * u).astype(BF16)
    out_ref[0] = x1 + g2_ref[0] * jnp.dot(z, wd_ref[...], preferred_element_type=F32)


def _ffn(x, o, wo, g1, nw, sh, sc, g2, wg, wu, wd, ts):
    b, s_len, d = x.shape
    f = wg.shape[1]
    row = lambda bb, t: (bb, 0, 0)
    tile = lambda bb, t: (bb, t, 0)
    vec = pl.BlockSpec((1, 1, d), row)
    return pl.pallas_call(
        _ffn_kernel,
        out_shape=jax.ShapeDtypeStruct((b, s_len, d), F32),
        grid=(b, s_len // ts),
        in_specs=[
            pl.BlockSpec((1, ts, d), tile),
            pl.BlockSpec((1, ts, V_WIDTH), tile),
            _const_spec((V_WIDTH, d)),
            vec, _const_spec((1, d)), vec, vec, vec,
            _const_spec((d, f)), _const_spec((d, f)), _const_spec((f, d)),
        ],
        out_specs=pl.BlockSpec((1, ts, d), tile),
        compiler_params=_cparams(("arbitrary", "arbitrary")),
        name="ffn",
    )(x, o, wo, g1, nw, sh, sc, g2, wg, wu, wd)


def _gmlp_kernel(x_ref, nw1_ref, sh1_ref, sc1_ref, g1_ref, win_ref, vnw_ref, vnb_ref, ws_ref, bs_ref, wout_ref,
                 nw2_ref, sh2_ref, sc2_ref, wrt_ref, x3_ref, h3_ref, idx_ref, gate_ref, t_scr, *, ts, half):
    x2 = x_ref[0]
    h = (_rms(x2, nw1_ref[...]) * (1.0 + sc1_ref[0]) + sh1_ref[0]).astype(BF16)
    z = jnp.dot(h, win_ref[...], preferred_element_type=F32)
    z = 0.5 * z * (1.0 + lax.erf(z * (2.0 ** -0.5)))
    u = z[:, 0:half]
    v = z[:, half:2 * half]
    mu = jnp.mean(v, axis=-1, keepdims=True)
    vc = v - mu
    v = vc * lax.rsqrt(jnp.mean(vc * vc, axis=-1, keepdims=True) + EPS) * vnw_ref[...] + vnb_ref[...]
    vb = v.astype(BF16)
    gd = half // GM_GROUPS
    for n in range(ts // GM_CHUNK):
        rows = slice(n * GM_CHUNK, (n + 1) * GM_CHUNK)
        for g in range(GM_GROUPS):
            cols = slice(g * gd, (g + 1) * gd)
            s = jnp.dot(ws_ref[g], vb[rows, cols], preferred_element_type=F32) + bs_ref[g]
            t_scr[rows, cols] = (u[rows, cols] * s).astype(BF16)
    x3 = x2 + g1_ref[0] * jnp.dot(t_scr[...], wout_ref[...], preferred_element_type=F32)
    x3_ref[0] = x3

    h3 = _rms(x3, nw2_ref[...]) * (1.0 + sc2_ref[0]) + sh2_ref[0]
    h3_ref[0] = h3
    nt = (((1,), (1,)), ((), ()))
    logits = lax.dot_general(wrt_ref[...], h3, nt, preferred_element_type=F32,
                             precision=lax.Precision.HIGHEST)
    ne = logits.shape[0]
    eid = lax.broadcasted_iota(jnp.int32, logits.shape, 0)
    m1 = jnp.max(logits, axis=0, keepdims=True)
    i1 = jnp.min(jnp.where(logits == m1, eid, ne), axis=0, keepdims=True)
    rest = jnp.where(eid == i1, -jnp.inf, logits)
    m2 = jnp.max(rest, axis=0, keepdims=True)
    i2 = jnp.min(jnp.where(rest == m2, eid, ne), axis=0, keepdims=True)
    e2 = jnp.exp(m2 - m1)
    den = 1.0 + e2
    idx_ref[0] = jnp.concatenate([i1, i2], axis=0)
    gate_ref[0] = jnp.concatenate([1.0 / den, e2 / den], axis=0)


def _gmlp(x, nw1, sh1, sc1, g1, win, vnw, vnb, ws, bs, wout, nw2, sh2, sc2, wrt, ts):
    b, s_len, d = x.shape
    half = wout.shape[0]
    ne = wrt.shape[0]
    assert ts % GM_CHUNK == 0 and s_len % ts == 0
    row = lambda bb, t: (bb, 0, 0)
    tile = lambda bb, t: (bb, t, 0)
    vec = pl.BlockSpec((1, 1, d), row)
    return pl.pallas_call(
        functools.partial(_gmlp_kernel, ts=ts, half=half),
        out_shape=(
            jax.ShapeDtypeStruct((b, s_len, d), F32),
            jax.ShapeDtypeStruct((b, s_len, d), F32),
            jax.ShapeDtypeStruct((b, TOP_K, s_len), jnp.int32),
            jax.ShapeDtypeStruct((b, TOP_K, s_len), F32),
        ),
        grid=(b, s_len // ts),
        in_specs=[
            pl.BlockSpec((1, ts, d), tile),
            _const_spec((1, d)), vec, vec, vec,
            _const_spec((d, 2 * half)),
            _const_spec((1, half)), _const_spec((1, half)),
            _const_spec((GM_GROUPS, GM_CHUNK, GM_CHUNK)),
            _const_spec((GM_GROUPS, GM_CHUNK, 1)),
            _const_spec((half, d)),
            _const_spec((1, d)), vec, vec,
            _const_spec((ne, d)),
        ],
        out_specs=(
            pl.BlockSpec((1, ts, d), tile),
            pl.BlockSpec((1, ts, d), tile),
            pl.BlockSpec((1, TOP_K, ts), lambda bb, t: (bb, 0, t)),
            pl.BlockSpec((1, TOP_K, ts), lambda bb, t: (bb, 0, t)),
        ),
        scratch_shapes=[pltpu.VMEM((ts, half), BF16)],
        compiler_params=_cparams(("arbitrary", "arbitrary")),
        name="gmlp",
    )(x, nw1, sh1, sc1, g1, win, vnw, vnb, ws, bs, wout, nw2, sh2, sc2, wrt)


def _route(idx, n_experts, tm):
    b, k, s_len = idx.shape
    n = b * s_len
    e_flat = jnp.swapaxes(idx, 0, 1).reshape(k * n)
    onehot = (e_flat[:, None] == jnp.arange(n_experts, dtype=jnp.int32)[None, :]).astype(jnp.int32)
    csum = jnp.cumsum(onehot, axis=0)
    counts = csum[-1]
    rank = jnp.take_along_axis(csum, e_flat[:, None], axis=1)[:, 0] - 1
    padded = ((counts + tm - 1) // tm) * tm
    ends = jnp.cumsum(padded)
    starts = ends - padded
    pos = (starts[e_flat] + rank).astype(jnp.int32)
    n_tiles = (k * n) // tm + n_experts + 1
    tile_ids = jnp.arange(n_tiles, dtype=jnp.int32)
    used = tile_ids * tm < ends[-1]
    tile_src = jnp.where(used, tile_ids, ends[-1] // tm - 1).astype(jnp.int32)
    tile_expert = jnp.minimum(jnp.searchsorted(ends, tile_src * tm, side="right"), n_experts - 1).astype(jnp.int32)
    order = jnp.argsort(e_flat, stable=True).astype(jnp.int32)
    slot_expert = jnp.repeat(tile_expert, tm)
    slot_rank = jnp.arange(n_tiles * tm, dtype=jnp.int32) - starts[slot_expert]
    valid = jnp.repeat(used, tm) & (slot_rank < counts[slot_expert])
    src = jnp.clip((jnp.cumsum(counts) - counts)[slot_expert] + slot_rank, 0, k * n - 1)
    slot_token = jnp.where(valid, order[src] % n, 0).astype(jnp.int32)
    return pos.reshape(k, n), slot_token, tile_expert, used.astype(jnp.int32)


def _experts_kernel(exp_ref, used_ref, tok_ref, tokn_ref, h_ref, wg_ref, wu_ref, wd_ref, y_ref,
                    xbuf, xb_scr, acc_scr, sems, *, tm, per_step):
    j = pl.program_id(0)
    f = pl.program_id(1)
    n_rows = per_step * pl.num_programs(1)
    used = used_ref[j] > 0
    slot = j % 2

    def row_copy(t_ref, r, sl):
        return pltpu.make_async_copy(h_ref.at[pl.ds(t_ref[0, 0, r], 1)], xbuf.at[sl, pl.ds(r, 1)], sems.at[sl])

    @pl.when(f == 0)
    def _():
        @pl.when(j == 0)
        def _():
            def body(r, carry):
                row_copy(tok_ref, r, 0).start()
                return carry
            lax.fori_loop(0, n_rows, body, 0, unroll=DMA_ISSUE_UNROLL)

        @pl.when(used_ref[jnp.maximum(j - 1, 0)] > 0)
        def _():
            pltpu.make_async_copy(h_ref.at[pl.ds(0, n_rows)], xbuf.at[slot, pl.ds(0, n_rows)], sems.at[slot]).wait()
            xb_scr[...] = xbuf[slot, 0:tm].astype(BF16)

        acc_scr[...] = jnp.zeros_like(acc_scr)

    @pl.when(used)
    def _():
        for i in range(per_step):
            row_copy(tokn_ref, f * per_step + i, 1 - slot).start()
        xb = xb_scr[...]
        a = jnp.dot(xb, wg_ref[0].astype(BF16), preferred_element_type=F32)
        u = jnp.dot(xb, wu_ref[0].astype(BF16), preferred_element_type=F32)
        z = (_silu(a) * u).astype(BF16)
        acc_scr[...] += jnp.dot(z, wd_ref[0].astype(BF16), preferred_element_type=F32)

    @pl.when(f == pl.num_programs(1) - 1)
    def _():
        y_ref[...] = acc_scr[...]


def _experts(h, slot_token, tile_expert, tile_used, wg, wu, wd, tm, tf):
    n, d = h.shape
    ne, _, f = wg.shape
    nf = f // tf
    n_tiles = tile_expert.shape[0]
    per_step = -(-tm // (nf * SUBLANES)) * SUBLANES
    n_rows = per_step * nf
    assert slot_token.shape[0] == n_tiles * tm and f % tf == 0 and n_rows <= n
    tok3 = jnp.pad(slot_token.reshape(n_tiles, 1, tm), ((0, 0), (0, 0), (0, n_rows - tm)))

    def fchunk(j, fi, used):
        return jnp.where(used[j] > 0, fi, nf - 1)

    smem_tile = functools.partial(pl.BlockSpec, (1, 1, n_rows), memory_space=pltpu.SMEM)
    return pl.pallas_call(
        functools.partial(_experts_kernel, tm=tm, per_step=per_step),
        out_shape=jax.ShapeDtypeStruct((n_tiles * tm, d), F32),
        grid_spec=pltpu.PrefetchScalarGridSpec(
            num_scalar_prefetch=2,
            grid=(n_tiles, nf),
            in_specs=[
                smem_tile(lambda j, fi, exp, used: (j, 0, 0)),
                smem_tile(lambda j, fi, exp, used: (jnp.minimum(j + 1, n_tiles - 1), 0, 0)),
                pl.BlockSpec(memory_space=pl.ANY),
                pl.BlockSpec((1, d, tf), lambda j, fi, exp, used: (exp[j], 0, fchunk(j, fi, used))),
                pl.BlockSpec((1, d, tf), lambda j, fi, exp, used: (exp[j], 0, fchunk(j, fi, used))),
                pl.BlockSpec((1, tf, d), lambda j, fi, exp, used: (exp[j], fchunk(j, fi, used), 0)),
            ],
            out_specs=pl.BlockSpec((tm, d), lambda j, fi, exp, used: (j, 0)),
            scratch_shapes=[pltpu.VMEM((2, n_rows, d), F32), pltpu.VMEM((tm, d), BF16),
                            pltpu.VMEM((tm, d), F32), pltpu.SemaphoreType.DMA((2,))],
        ),
        compiler_params=_cparams(("arbitrary", "arbitrary")),
        name="moe_experts",
    )(tile_expert, tile_used, tok3, tok3, h, wg, wu, wd)


def _combine_kernel(pos_ref, posn_ref, x_ref, gate_ref, g2_ref, fw_ref, ys_ref, out_ref, ybuf, sems, *, rows):
    i = pl.program_id(0)
    slot = i % 2

    def issue(p_ref, sl):
        def body(r, carry):
            for k in range(TOP_K):
                pltpu.make_async_copy(ys_ref.at[pl.ds(p_ref[0, k, r], 1)], ybuf.at[sl, pl.ds(k * rows + r, 1)],
                                      sems.at[sl]).start()
            return carry
        lax.fori_loop(0, rows, body, 0, unroll=DMA_ISSUE_UNROLL)

    @pl.when(i == 0)
    def _():
        issue(pos_ref, 0)

    @pl.when(i + 1 < pl.num_programs(0))
    def _():
        issue(posn_ref, 1 - slot)

    pltpu.make_async_copy(ys_ref.at[pl.ds(0, TOP_K * rows)], ybuf.at[slot], sems.at[slot]).wait()
    g = gate_ref[...]
    y = g[:, 0:1] * ybuf[slot, 0:rows] + g[:, 1:2] * ybuf[slot, rows:2 * rows]
    out_ref[...] = _rms(x_ref[...] + g2_ref[0] * y, fw_ref[...])


def _combine(x, ys, pos, gates, g2, fw, rows):
    n, d = x.shape
    n_steps = n // rows
    steps_per_batch = n_steps // g2.shape[0]
    assert n % rows == 0 and n_steps % g2.shape[0] == 0
    pos3 = jnp.swapaxes(pos.reshape(TOP_K, n_steps, rows), 0, 1)
    smem_tile = functools.partial(pl.BlockSpec, (1, TOP_K, rows), memory_space=pltpu.SMEM)
    return pl.pallas_call(
        functools.partial(_combine_kernel, rows=rows),
        out_shape=jax.ShapeDtypeStruct((n, d), F32),
        grid=(n_steps,),
        in_specs=[
            smem_tile(lambda i: (i, 0, 0)),
            smem_tile(lambda i: (jnp.minimum(i + 1, n_steps - 1), 0, 0)),
            pl.BlockSpec((rows, d), lambda i: (i, 0)),
            pl.BlockSpec((rows, TOP_K), lambda i: (i, 0)),
            pl.BlockSpec((1, 1, d), lambda i: (i // steps_per_batch, 0, 0)),
            pl.BlockSpec((1, d), lambda i: (0, 0)),
            pl.BlockSpec(memory_space=pl.ANY),
        ],
        out_specs=pl.BlockSpec((rows, d), lambda i: (i, 0)),
        scratch_shapes=[pltpu.VMEM((2, TOP_K * rows, d), F32), pltpu.SemaphoreType.DMA((2,))],
        compiler_params=_cparams(("arbitrary",)),
        name="moe_combine",
    )(pos3, pos3, x, gates, g2, fw, ys)


def _pick(n, prefs):
    for p in prefs:
        if n % p == 0:
            return p
    return n


def kernel(x, c, ctx, c_ctx, w_mod, b_mod, norm1_w, norm2_w, final_norm_w, da_w_qkv, da_w_o, da_lambda_q1, da_lambda_k1, da_lambda_q2, da_lambda_k2, da_subln_w, gm_w_in, gm_vnorm_w, gm_vnorm_b, gm_w_s, gm_b_s, gm_w_out, ffn_w_gate, ffn_w_up, ffn_w_down, moe_w_router, moe_w_gate, moe_w_up, moe_w_down):
    b, s_len, d = x.shape
    ctx_len = ctx.shape[1]

    pad = (-(b + 1)) % SUBLANES
    cond = jnp.concatenate([c, c_ctx[None, :], jnp.zeros((pad, d), F32)], axis=0)
    mod = _ada(cond, w_mod, b_mod)

    def chunks(layer, r0, r1):
        m = mod[layer, r0:r1]
        return [m[:, None, i * d:(i + 1) * d] for i in range(6)]

    sh1, sc1, g1, sh2, sc2, g2 = chunks(0, 0, b)
    csh1, csc1 = chunks(0, b, b + 1)[:2]
    sh1b, sc1b, g1b, sh2b, sc2b, g2b = chunks(1, 0, b)

    ts = _pick(ctx_len, ROW_TILES[1:])
    wq = da_w_qkv[0, :, 0:QK_WIDTH]
    wk = da_w_qkv[0, :, QK_WIDTH:2 * QK_WIDTH]
    wv = da_w_qkv[0, :, 2 * QK_WIDTH:]
    k_all, qt, vt5 = _qkv(x, ctx, norm1_w[0][None], sh1, sc1, csh1, csc1,
                          wq.T.astype(BF16), wk.astype(BF16), wv.T.astype(BF16), ts)
    lam_init = 0.8 - 0.6 * math.exp(-0.3 * 0)
    o = _attention(qt, k_all, vt5, da_lambda_q1[0][None], da_lambda_k1[0][None], da_lambda_q2[0][None],
                   da_lambda_k2[0][None], da_subln_w[0][:, None], lam_init, _pick(s_len, (1024,) + ROW_TILES))
    x2 = _ffn(x, o, da_w_o[0].astype(BF16), g1, norm2_w[0][None], sh2, sc2, g2,
              ffn_w_gate[0].astype(BF16), ffn_w_up[0].astype(BF16), ffn_w_down[0].astype(BF16),
              _pick(s_len, ROW_TILES))

    x3, h3, idx, gates = _gmlp(x2, norm1_w[1][None], sh1b, sc1b, g1b, gm_w_in[0].astype(BF16),
                               gm_vnorm_w[0][None], gm_vnorm_b[0][None], gm_w_s[0].astype(BF16),
                               gm_b_s[0][:, :, None], gm_w_out[0].astype(BF16),
                               norm2_w[1][None], sh2b, sc2b, moe_w_router[0].T, _pick(s_len, ROW_TILES))
    n = b * s_len
    ne, _, f = moe_w_gate[0].shape
    tm = _pick(TOP_K * n, (1024,) + ROW_TILES)
    pos, slot_token, tile_expert, tile_used = _route(idx, ne, tm)
    ys = _experts(h3.reshape(n, d), slot_token, tile_expert, tile_used, moe_w_gate[0], moe_w_up[0], moe_w_down[0],
                  tm, _pick(f, ROW_TILES))
    gates_t = jnp.swapaxes(gates, 1, 2).reshape(n, TOP_K)
    out = _combine(x3.reshape(n, d), ys, pos, gates_t, g2b, final_norm_w[None], _pick(s_len, ROW_TILES[1:]))
    return out.reshape(b, s_len, d)
```

```python
import functools
import math

import jax
import jax.numpy as jnp
from jax import lax
from jax.experimental import pallas as pl
from jax.experimental.pallas import tpu as pltpu

F32 = jnp.float32
BF16 = jnp.bfloat16

EPS = 1e-6
GRID_W = 64
ROPE_BASE = 10000.0
N_HEADS = 8
HEAD_DIM = 64
ROPE_FREQS = HEAD_DIM // 4
V_DIM = 2 * HEAD_DIM
QK_WIDTH = N_HEADS * 2 * HEAD_DIM
V_WIDTH = N_HEADS * V_DIM
V_ROWS = V_DIM + 16
KEY_TILES_PER_BLOCK = (5, 4, 3, 2)
SCORE_BUFFERS = 2
DMA_ISSUE_UNROLL = 4
ROW_TILES = (512, 256, 128)
GM_CHUNK = 128
GM_GROUPS = 8
TOP_K = 2
NEG_BIG = -1e30
LOG2_E = math.log2(math.e)

LANES = 128
SUBLANES = 8
VMEM_LIMIT_BYTES = 56 * 1024 * 1024


def _cparams(semantics):
    return pltpu.CompilerParams(dimension_semantics=semantics, vmem_limit_bytes=VMEM_LIMIT_BYTES)


def _rms(x, w):
    return x * lax.rsqrt(jnp.mean(x * x, axis=-1, keepdims=True) + EPS) * w


def _silu(x):
    return x * jax.nn.sigmoid(x)


def _const_spec(shape):
    nd = len(shape)
    return pl.BlockSpec(shape, lambda *_: (0,) * nd, pipeline_mode=pl.Buffered(1))


def _ada_kernel(cond_ref, w_ref, b_ref, o_ref):
    s = _silu(cond_ref[...])
    o_ref[0] = jnp.dot(s, w_ref[0], preferred_element_type=F32, precision=lax.Precision.HIGHEST) + b_ref[0]


def _ada(cond, w_mod, b_mod):
    depth, d, n = w_mod.shape
    rows = cond.shape[0]
    tn = n // 4
    return pl.pallas_call(
        _ada_kernel,
        out_shape=jax.ShapeDtypeStruct((depth, rows, n), F32),
        grid=(depth, n // tn),
        in_specs=[
            pl.BlockSpec((rows, d), lambda l, j: (0, 0)),
            pl.BlockSpec((1, d, tn), lambda l, j: (l, 0, j)),
            pl.BlockSpec((1, 1, tn), lambda l, j: (l, 0, j)),
        ],
        out_specs=pl.BlockSpec((1, rows, tn), lambda l, j: (l, 0, j)),
        compiler_params=_cparams(("arbitrary", "arbitrary")),
        name="ada",
    )(cond, w_mod, b_mod.reshape(depth, 1, n))


def _qkv_kernel(x_ref, ctx_ref, nw_ref, sh_ref, sc_ref, csh_ref, csc_ref, wqt_ref, wk_ref, wvt_ref,
                ck_ref, sk_ref, cq_ref, sq_ref, k_ref, qt_ref, vt_ref, *, n_ctx, ts):
    is_ctx = pl.program_id(1) < n_ctx
    xin = jnp.where(is_ctx, ctx_ref[0], x_ref[0])
    sh = jnp.where(is_ctx, csh_ref[0], sh_ref[0])
    sc = jnp.where(is_ctx, csc_ref[0], sc_ref[0])
    h = (_rms(xin, nw_ref[...]) * (1.0 + sc) + sh).astype(BF16)

    k = jnp.dot(h, wk_ref[...], preferred_element_type=F32)
    lane = lax.broadcasted_iota(jnp.int32, (ts, LANES), 1)
    lower = (lane % 32) < 16
    ck = ck_ref[...]
    sk = sk_ref[...]
    for j in range(N_HEADS):
        kb = k[:, j * LANES:(j + 1) * LANES]
        partner = jnp.where(lower, pltpu.roll(kb, LANES - 16, 1), pltpu.roll(kb, 16, 1))
        k_ref[0, :, j * LANES:(j + 1) * LANES] = (kb * ck + partner * sk).astype(BF16)

    nt = (((1,), (1,)), ((), ()))
    qt = lax.dot_general(wqt_ref[...], h, nt, preferred_element_type=F32)
    cq = cq_ref[...]
    sq = sq_ref[...]
    scale = HEAD_DIM ** -0.5 * LOG2_E
    for g in range(QK_WIDTH // HEAD_DIM):
        blk = qt[g * HEAD_DIM:(g + 1) * HEAD_DIM]
        partner = jnp.concatenate([blk[16:32], blk[0:16], blk[48:64], blk[32:48]], axis=0)
        qt_ref[0, g * HEAD_DIM:(g + 1) * HEAD_DIM, :] = ((blk * cq + partner * sq) * scale).astype(BF16)

    vt = lax.dot_general(wvt_ref[...], h, nt, preferred_element_type=F32)
    ones_row = (lax.broadcasted_iota(jnp.int32, (V_ROWS - V_DIM, ts), 0) == 0).astype(BF16)
    for hh in range(N_HEADS):
        vt_ref[0, hh, 0, 0:V_DIM, :] = vt[hh * V_DIM:(hh + 1) * V_DIM].astype(BF16)
        vt_ref[0, hh, 0, V_DIM:V_ROWS, :] = ones_row


def _rope_tables(s_len, ctx_len):
    pos = jnp.arange(s_len)
    row = (pos // GRID_W).astype(F32)
    col = (pos % GRID_W).astype(F32)
    inv_freq = ROPE_BASE ** (-jnp.arange(ROPE_FREQS, dtype=F32) / ROPE_FREQS)
    ar = row[:, None] * inv_freq
    ac = col[:, None] * inv_freq
    cos64 = jnp.concatenate([jnp.cos(ar), jnp.cos(ar), jnp.cos(ac), jnp.cos(ac)], axis=-1)
    sin64 = jnp.concatenate([-jnp.sin(ar), jnp.sin(ar), -jnp.sin(ac), jnp.sin(ac)], axis=-1)
    ck = jnp.concatenate([jnp.ones((ctx_len, LANES), F32), jnp.tile(cos64, (1, 2))], axis=0)
    sk = jnp.concatenate([jnp.zeros((ctx_len, LANES), F32), jnp.tile(sin64, (1, 2))], axis=0)
    return ck, sk, cos64.T, sin64.T


def _qkv(x, ctx, nw, sh, sc, csh, csc, wqt, wk, wvt, ts):
    b, s_len, d = x.shape
    ctx_len = ctx.shape[1]
    assert s_len % ts == 0 and ctx_len % ts == 0
    n_ctx = ctx_len // ts
    n_lat = s_len // ts
    ktot = ctx_len + s_len
    sub = _pick(n_ctx + n_lat, KEY_TILES_PER_BLOCK)
    ck, sk, cq, sq = _rope_tables(s_len, ctx_len)

    def lat(t):
        return jnp.maximum(t - n_ctx, 0)

    row = lambda bb, t: (bb, 0, 0)
    return pl.pallas_call(
        functools.partial(_qkv_kernel, n_ctx=n_ctx, ts=ts),
        out_shape=(
            jax.ShapeDtypeStruct((b, ktot, QK_WIDTH), BF16),
            jax.ShapeDtypeStruct((b, QK_WIDTH, s_len), BF16),
            jax.ShapeDtypeStruct((b, N_HEADS, ktot // (ts * sub), V_ROWS, ts * sub), BF16),
        ),
        grid=(b, n_ctx + n_lat),
        in_specs=[
            pl.BlockSpec((1, ts, d), lambda bb, t: (bb, lat(t), 0)),
            pl.BlockSpec((1, ts, d), lambda bb, t: (bb, jnp.minimum(t, n_ctx - 1), 0)),
            _const_spec((1, d)),
            pl.BlockSpec((1, 1, d), row),
            pl.BlockSpec((1, 1, d), row),
            _const_spec((1, 1, d)),
            _const_spec((1, 1, d)),
            _const_spec((QK_WIDTH, d)),
            _const_spec((d, QK_WIDTH)),
            _const_spec((V_WIDTH, d)),
            pl.BlockSpec((ts, LANES), lambda bb, t: (t, 0)),
            pl.BlockSpec((ts, LANES), lambda bb, t: (t, 0)),
            pl.BlockSpec((HEAD_DIM, ts), lambda bb, t: (0, lat(t))),
            pl.BlockSpec((HEAD_DIM, ts), lambda bb, t: (0, lat(t))),
        ],
        out_specs=(
            pl.BlockSpec((1, ts, QK_WIDTH), lambda bb, t: (bb, t, 0)),
            pl.BlockSpec((1, QK_WIDTH, ts), lambda bb, t: (bb, 0, lat(t))),
            pl.BlockSpec((1, N_HEADS, 1, V_ROWS, ts), lambda bb, t: (bb, 0, t // sub, 0, t % sub)),
        ),
        compiler_params=_cparams(("arbitrary", "arbitrary")),
        name="qkv",
    )(x, ctx, nw, sh, sc, csh, csc, wqt, wk, wvt, ck, sk, cq, sq)


def _attn_kernel(lq1_ref, lk1_ref, lq2_ref, lk2_ref, subw_ref, q_ref, k_ref, v_ref, o_ref, w_scr, acc_scr,
                 *bufs, tq, tkb, nb, lam_init):
    nbuf = len(bufs) // 2
    s_bufs, p_bufs = bufs[:nbuf], bufs[nbuf:]
    q = q_ref[0]
    zero = jnp.zeros((HEAD_DIM, tq), BF16)
    w_scr[0:HEAD_DIM, 0:tq] = q[0:HEAD_DIM]
    w_scr[0:HEAD_DIM, tq:2 * tq] = zero
    w_scr[HEAD_DIM:2 * HEAD_DIM, 0:tq] = zero
    w_scr[HEAD_DIM:2 * HEAD_DIM, tq:2 * tq] = q[HEAD_DIM:2 * HEAD_DIM]
    acc_scr[...] = jnp.zeros_like(acc_scr)

    def score_block(blk, slot):
        off = pl.multiple_of(blk * tkb, tkb)
        s = jnp.dot(k_ref[0, pl.ds(off, tkb), :], w_scr[...], preferred_element_type=F32)
        s_bufs[slot][...] = s
        return jnp.max(s, axis=0, keepdims=True)

    def softmax(slot, m, block_max):
        m_new = jnp.maximum(m, block_max)
        p_bufs[slot][...] = jnp.exp2(s_bufs[slot][...] - m_new).astype(BF16)
        return m_new, jnp.exp2(m - m_new)

    def values(j, slot, alpha):
        pv = jnp.dot(v_ref[0, 0, j], p_bufs[slot][...], preferred_element_type=F32)
        acc_scr[...] = acc_scr[...] * alpha + pv

    def step(j, slot, m, alpha, block_max):
        next_max = score_block(j + 2, (slot + 2) % nbuf)
        m, alpha_next = softmax((slot + 1) % nbuf, m, block_max)
        values(j, slot, alpha)
        return m, alpha_next, next_max

    m, alpha = softmax(0, jnp.full((1, 2 * tq), NEG_BIG, F32), score_block(0, 0))
    block_max = score_block(1, 1) if nb > 1 else None
    n_full = max(nb - 2, 0)

    def unrolled(i, carry):
        for r in range(nbuf):
            carry = step(nbuf * i + r, r, *carry)
        return carry

    if n_full:
        m, alpha, block_max = lax.fori_loop(0, n_full // nbuf, unrolled, (m, alpha, block_max))
    for j in range(nbuf * (n_full // nbuf), n_full):
        m, alpha, block_max = step(j, j % nbuf, m, alpha, block_max)
    if nb > 1:
        m, alpha_next = softmax((nb - 1) % nbuf, m, block_max)
        values(nb - 2, (nb - 2) % nbuf, alpha)
        alpha = alpha_next
    values(nb - 1, (nb - 1) % nbuf, alpha)

    lam = (jnp.exp(jnp.sum(lq1_ref[...] * lk1_ref[...], axis=-1, keepdims=True))
           - jnp.exp(jnp.sum(lq2_ref[...] * lk2_ref[...], axis=-1, keepdims=True)) + lam_init)
    acc = acc_scr[0:V_DIM, :]
    inv = 1.0 / acc_scr[V_DIM:V_DIM + 1, :]
    o = acc[:, 0:tq] * inv[:, 0:tq] - lam * (acc[:, tq:2 * tq] * inv[:, tq:2 * tq])
    y = o * lax.rsqrt(jnp.mean(o * o, axis=0, keepdims=True) + EPS) * subw_ref[...]
    y = y * (1.0 - lam_init)
    o_ref[0] = y.T.astype(BF16)


def _attention(qt, k_all, vt5, lq1, lk1, lq2, lk2, subw, lam_init, tq):
    b, _, s_len = qt.shape
    ktot = k_all.shape[1]
    nb, tkb = vt5.shape[2], vt5.shape[4]
    assert s_len % tq == 0 and nb * tkb == ktot and vt5.shape[3] == V_ROWS
    vec = _const_spec((1, HEAD_DIM))
    return pl.pallas_call(
        functools.partial(_attn_kernel, tq=tq, tkb=tkb, nb=nb, lam_init=lam_init),
        out_shape=jax.ShapeDtypeStruct((b, s_len, V_WIDTH), BF16),
        grid=(b, N_HEADS, s_len // tq),
        in_specs=[
            vec, vec, vec, vec,
            _const_spec((V_DIM, 1)),
            pl.BlockSpec((1, 2 * HEAD_DIM, tq), lambda bb, hh, i: (bb, hh, i)),
            pl.BlockSpec((1, ktot, 2 * HEAD_DIM), lambda bb, hh, i: (bb, 0, hh), pipeline_mode=pl.Buffered(1)),
            pl.BlockSpec((1, 1, nb, V_ROWS, tkb), lambda bb, hh, i: (bb, hh, 0, 0, 0),
                         pipeline_mode=pl.Buffered(1)),
        ],
        out_specs=pl.BlockSpec((1, tq, V_DIM), lambda bb, hh, i: (bb, i, hh)),
        scratch_shapes=[
            pltpu.VMEM((2 * HEAD_DIM, 2 * tq), BF16),
            pltpu.VMEM((V_ROWS, 2 * tq), F32),
        ] + [pltpu.VMEM((tkb, 2 * tq), F32)] * SCORE_BUFFERS + [pltpu.VMEM((tkb, 2 * tq), BF16)] * SCORE_BUFFERS,
        compiler_params=_cparams(("arbitrary", "arbitrary", "arbitrary")),
        name="attention",
    )(lq1, lk1, lq2, lk2, subw, qt, k_all, vt5)


def _ffn_kernel(x_ref, o_ref, wo_ref, g1_ref, nw_ref, sh_ref, sc_ref, g2_ref, wg_ref, wu_ref, wd_ref, out_ref):
    y = jnp.dot(o_ref[0], wo_ref[...], preferred_element_type=F32)
    x1 = x_ref[0] + g1_ref[0] * y
    h = (_rms(x1, nw_ref[...]) * (1.0 + sc_ref[0]) + sh_ref[0]).astype(BF16)
    a = jnp.dot(h, wg_ref[...], preferred_element_type=F32)
    u = jnp.dot(h, wu_ref[...], preferred_element_type=F32)
    z = (_silu(a) * u).astype(BF16)
    out_ref[0] = x1 + g2_ref[0] * jnp.dot(z, wd_ref[...], preferred_element_type=F32)


def _ffn(x, o, wo, g1, nw, sh, sc, g2, wg, wu, wd, ts):
    b, s_len, d = x.shape
    f = wg.shape[1]
    row = lambda bb, t: (bb, 0, 0)
    tile = lambda bb, t: (bb, t, 0)
    vec = pl.BlockSpec((1, 1, d), row)
    return pl.pallas_call(
        _ffn_kernel,
        out_shape=jax.ShapeDtypeStruct((b, s_len, d), F32),
        grid=(b, s_len // ts),
        in_specs=[
            pl.BlockSpec((1, ts, d), tile),
            pl.BlockSpec((1, ts, V_WIDTH), tile),
            _const_spec((V_WIDTH, d)),
            vec, _const_spec((1, d)), vec, vec, vec,
            _const_spec((d, f)), _const_spec((d, f)), _const_spec((f, d)),
        ],
        out_specs=pl.BlockSpec((1, ts, d), tile),
        compiler_params=_cparams(("arbitrary", "arbitrary")),
        name="ffn",
    )(x, o, wo, g1, nw, sh, sc, g2, wg, wu, wd)


def _gmlp_kernel(x_ref, nw1_ref, sh1_ref, sc1_ref, g1_ref, win_ref, vnw_ref, vnb_ref, ws_ref, bs_ref, wout_ref,
                 nw2_ref, sh2_ref, sc2_ref, wrt_ref, x3_ref, h3_ref, idx_ref, gate_ref, t_scr, *, ts, half):
    x2 = x_ref[0]
    h = (_rms(x2, nw1_ref[...]) * (1.0 + sc1_ref[0]) + sh1_ref[0]).astype(BF16)
    z = jnp.dot(h, win_ref[...], preferred_element_type=F32)
    z = 0.5 * z * (1.0 + lax.erf(z * (2.0 ** -0.5)))
    u = z[:, 0:half]
    v = z[:, half:2 * half]
    mu = jnp.mean(v, axis=-1, keepdims=True)
    vc = v - mu
    v = vc * lax.rsqrt(jnp.mean(vc * vc, axis=-1, keepdims=True) + EPS) * vnw_ref[...] + vnb_ref[...]
    vb = v.astype(BF16)
    gd = half // GM_GROUPS
    for n in range(ts // GM_CHUNK):
        rows = slice(n * GM_CHUNK, (n + 1) * GM_CHUNK)
        for g in range(GM_GROUPS):
            cols = slice(g * gd, (g + 1) * gd)
            s = jnp.dot(ws_ref[g], vb[rows, cols], preferred_element_type=F32) + bs_ref[g]
            t_scr[rows, cols] = (u[rows, cols] * s).astype(BF16)
    x3 = x2 + g1_ref[0] * jnp.dot(t_scr[...], wout_ref[...], preferred_element_type=F32)
    x3_ref[0] = x3

    h3 = _rms(x3, nw2_ref[...]) * (1.0 + sc2_ref[0]) + sh2_ref[0]
    h3_ref[0] = h3
    nt = (((1,), (1,)), ((), ()))
    logits = lax.dot_general(wrt_ref[...], h3, nt, preferred_element_type=F32,
                             precision=lax.Precision.HIGHEST)
    ne = logits.shape[0]
    eid = lax.broadcasted_iota(jnp.int32, logits.shape, 0)
    m1 = jnp.max(logits, axis=0, keepdims=True)
    i1 = jnp.min(jnp.where(logits == m1, eid, ne), axis=0, keepdims=True)
    rest = jnp.where(eid == i1, -jnp.inf, logits)
    m2 = jnp.max(rest, axis=0, keepdims=True)
    i2 = jnp.min(jnp.where(rest == m2, eid, ne), axis=0, keepdims=True)
    e2 = jnp.exp(m2 - m1)
    den = 1.0 + e2
    idx_ref[0] = jnp.concatenate([i1, i2], axis=0)
    gate_ref[0] = jnp.concatenate([1.0 / den, e2 / den], axis=0)


def _gmlp(x, nw1, sh1, sc1, g1, win, vnw, vnb, ws, bs, wout, nw2, sh2, sc2, wrt, ts):
    b, s_len, d = x.shape
    half = wout.shape[0]
    ne = wrt.shape[0]
    assert ts % GM_CHUNK == 0 and s_len % ts == 0
    row = lambda bb, t: (bb, 0, 0)
    tile = lambda bb, t: (bb, t, 0)
    vec = pl.BlockSpec((1, 1, d), row)
    return pl.pallas_call(
        functools.partial(_gmlp_kernel, ts=ts, half=half),
        out_shape=(
            jax.ShapeDtypeStruct((b, s_len, d), F32),
            jax.ShapeDtypeStruct((b, s_len, d), F32),
            jax.ShapeDtypeStruct((b, TOP_K, s_len), jnp.int32),
            jax.ShapeDtypeStruct((b, TOP_K, s_len), F32),
        ),
        grid=(b, s_len // ts),
        in_specs=[
            pl.BlockSpec((1, ts, d), tile),
            _const_spec((1, d)), vec, vec, vec,
            _const_spec((d, 2 * half)),
            _const_spec((1, half)), _const_spec((1, half)),
            _const_spec((GM_GROUPS, GM_CHUNK, GM_CHUNK)),
            _const_spec((GM_GROUPS, GM_CHUNK, 1)),
            _const_spec((half, d)),
            _const_spec((1, d)), vec, vec,
            _const_spec((ne, d)),
        ],
        out_specs=(
            pl.BlockSpec((1, ts, d), tile),
            pl.BlockSpec((1, ts, d), tile),
            pl.BlockSpec((1, TOP_K, ts), lambda bb, t: (bb, 0, t)),
            pl.BlockSpec((1, TOP_K, ts), lambda bb, t: (bb, 0, t)),
        ),
        scratch_shapes=[pltpu.VMEM((ts, half), BF16)],
        compiler_params=_cparams(("arbitrary", "arbitrary")),
        name="gmlp",
    )(x, nw1, sh1, sc1, g1, win, vnw, vnb, ws, bs, wout, nw2, sh2, sc2, wrt)


def _route(idx, n_experts, tm):
    b, k, s_len = idx.shape
    n = b * s_len
    e_flat = jnp.swapaxes(idx, 0, 1).reshape(k * n)
    onehot = (e_flat[:, None] == jnp.arange(n_experts, dtype=jnp.int32)[None, :]).astype(jnp.int32)
    csum = jnp.cumsum(onehot, axis=0)
    counts = csum[-1]
    rank = jnp.take_along_axis(csum, e_flat[:, None], axis=1)[:, 0] - 1
    padded = ((counts + tm - 1) // tm) * tm
    ends = jnp.cumsum(padded)
    starts = ends - padded
    pos = (starts[e_flat] + rank).astype(jnp.int32)
    n_tiles = (k * n) // tm + n_experts + 1
    tile_ids = jnp.arange(n_tiles, dtype=jnp.int32)
    used = tile_ids * tm < ends[-1]
    tile_src = jnp.where(used, tile_ids, ends[-1] // tm - 1).astype(jnp.int32)
    tile_expert = jnp.minimum(jnp.searchsorted(ends, tile_src * tm, side="right"), n_experts - 1).astype(jnp.int32)
    order = jnp.argsort(e_flat, stable=True).astype(jnp.int32)
    slot_expert = jnp.repeat(tile_expert, tm)
    slot_rank = jnp.arange(n_tiles * tm, dtype=jnp.int32) - starts[slot_expert]
    valid = jnp.repeat(used, tm) & (slot_rank < counts[slot_expert])
    src = jnp.clip((jnp.cumsum(counts) - counts)[slot_expert] + slot_rank, 0, k * n - 1)
    slot_token = jnp.where(valid, order[src] % n, 0).astype(jnp.int32)
    return pos.reshape(k, n), slot_token, tile_expert, used.astype(jnp.int32)


def _experts_kernel(exp_ref, used_ref, tok_ref, tokn_ref, h_ref, wg_ref, wu_ref, wd_ref, y_ref,
                    xbuf, xb_scr, acc_scr, sems, *, tm, per_step):
    j = pl.program_id(0)
    f = pl.program_id(1)
    n_rows = per_step * pl.num_programs(1)
    used = used_ref[j] > 0
    slot = j % 2

    def row_copy(t_ref, r, sl):
        return pltpu.make_async_copy(h_ref.at[pl.ds(t_ref[0, 0, r], 1)], xbuf.at[sl, pl.ds(r, 1)], sems.at[sl])

    @pl.when(f == 0)
    def _():
        @pl.when(j == 0)
        def _():
            def body(r, carry):
                row_copy(tok_ref, r, 0).start()
                return carry
            lax.fori_loop(0, n_rows, body, 0, unroll=DMA_ISSUE_UNROLL)

        @pl.when(used_ref[jnp.maximum(j - 1, 0)] > 0)
        def _():
            pltpu.make_async_copy(h_ref.at[pl.ds(0, n_rows)], xbuf.at[slot, pl.ds(0, n_rows)], sems.at[slot]).wait()
            xb_scr[...] = xbuf[slot, 0:tm].astype(BF16)

        acc_scr[...] = jnp.zeros_like(acc_scr)

    @pl.when(used)
    def _():
        for i in range(per_step):
            row_copy(tokn_ref, f * per_step + i, 1 - slot).start()
        xb = xb_scr[...]
        a = jnp.dot(xb, wg_ref[0].astype(BF16), preferred_element_type=F32)
        u = jnp.dot(xb, wu_ref[0].astype(BF16), preferred_element_type=F32)
        z = (_silu(a) * u).astype(BF16)
        acc_scr[...] += jnp.dot(z, wd_ref[0].astype(BF16), preferred_element_type=F32)

    @pl.when(f == pl.num_programs(1) - 1)
    def _():
        y_ref[...] = acc_scr[...]


def _experts(h, slot_token, tile_expert, tile_used, wg, wu, wd, tm, tf):
    n, d = h.shape
    ne, _, f = wg.shape
    nf = f // tf
    n_tiles = tile_expert.shape[0]
    per_step = -(-tm // (nf * SUBLANES)) * SUBLANES
    n_rows = per_step * nf
    assert slot_token.shape[0] == n_tiles * tm and f % tf == 0 and n_rows <= n
    tok3 = jnp.pad(slot_token.reshape(n_tiles, 1, tm), ((0, 0), (0, 0), (0, n_rows - tm)))

    def fchunk(j, fi, used):
        return jnp.where(used[j] > 0, fi, nf - 1)

    smem_tile = functools.partial(pl.BlockSpec, (1, 1, n_rows), memory_space=pltpu.SMEM)
    return pl.pallas_call(
        functools.partial(_experts_kernel, tm=tm, per_step=per_step),
        out_shape=jax.ShapeDtypeStruct((n_tiles * tm, d), F32),
        grid_spec=pltpu.PrefetchScalarGridSpec(
            num_scalar_prefetch=2,
            grid=(n_tiles, nf),
            in_specs=[
                smem_tile(lambda j, fi, exp, used: (j, 0, 0)),
                smem_tile(lambda j, fi, exp, used: (jnp.minimum(j + 1, n_tiles - 1), 0, 0)),
                pl.BlockSpec(memory_space=pl.ANY),
                pl.BlockSpec((1, d, tf), lambda j, fi, exp, used: (exp[j], 0, fchunk(j, fi, used))),
                pl.BlockSpec((1, d, tf), lambda j, fi, exp, used: (exp[j], 0, fchunk(j, fi, used))),
                pl.BlockSpec((1, tf, d), lambda j, fi, exp, used: (exp[j], fchunk(j, fi, used), 0)),
            ],
            out_specs=pl.BlockSpec((tm, d), lambda j, fi, exp, used: (j, 0)),
            scratch_shapes=[pltpu.VMEM((2, n_rows, d), F32), pltpu.VMEM((tm, d), BF16),
                            pltpu.VMEM((tm, d), F32), pltpu.SemaphoreType.DMA((2,))],
        ),
        compiler_params=_cparams(("arbitrary", "arbitrary")),
        name="moe_experts",
    )(tile_expert, tile_used, tok3, tok3, h, wg, wu, wd)


def _combine_kernel(pos_ref, posn_ref, x_ref, gate_ref, g2_ref, fw_ref, ys_ref, out_ref, ybuf, sems, *, rows):
    i = pl.program_id(0)
    slot = i % 2

    def issue(p_ref, sl):
        def body(r, carry):
            for k in range(TOP_K):
                pltpu.make_async_copy(ys_ref.at[pl.ds(p_ref[0, k, r], 1)], ybuf.at[sl, pl.ds(k * rows + r, 1)],
                                      sems.at[sl]).start()
            return carry
        lax.fori_loop(0, rows, body, 0, unroll=DMA_ISSUE_UNROLL)

    @pl.when(i == 0)
    def _():
        issue(pos_ref, 0)

    @pl.when(i + 1 < pl.num_programs(0))
    def _():
        issue(posn_ref, 1 - slot)

    pltpu.make_async_copy(ys_ref.at[pl.ds(0, TOP_K * rows)], ybuf.at[slot], sems.at[slot]).wait()
    g = gate_ref[...]
    y = g[:, 0:1] * ybuf[slot, 0:rows] + g[:, 1:2] * ybuf[slot, rows:2 * rows]
    out_ref[...] = _rms(x_ref[...] + g2_ref[0] * y, fw_ref[...])


def _combine(x, ys, pos, gates, g2, fw, rows):
    n, d = x.shape
    n_steps = n // rows
    steps_per_batch = n_steps // g2.shape[0]
    assert n % rows == 0 and n_steps % g2.shape[0] == 0
    pos3 = jnp.swapaxes(pos.reshape(TOP_K, n_steps, rows), 0, 1)
    smem_tile = functools.partial(pl.BlockSpec, (1, TOP_K, rows), memory_space=pltpu.SMEM)
    return pl.pallas_call(
        functools.partial(_combine_kernel, rows=rows),
        out_shape=jax.ShapeDtypeStruct((n, d), F32),
        grid=(n_steps,),
        in_specs=[
            smem_tile(lambda i: (i, 0, 0)),
            smem_tile(lambda i: (jnp.minimum(i + 1, n_steps - 1), 0, 0)),
            pl.BlockSpec((rows, d), lambda i: (i, 0)),
            pl.BlockSpec((rows, TOP_K), lambda i: (i, 0)),
            pl.BlockSpec((1, 1, d), lambda i: (i // steps_per_batch, 0, 0)),
            pl.BlockSpec((1, d), lambda i: (0, 0)),
            pl.BlockSpec(memory_space=pl.ANY),
        ],
        out_specs=pl.BlockSpec((rows, d), lambda i: (i, 0)),
        scratch_shapes=[pltpu.VMEM((2, TOP_K * rows, d), F32), pltpu.SemaphoreType.DMA((2,))],
        compiler_params=_cparams(("arbitrary",)),
        name="moe_combine",
    )(pos3, pos3, x, gates, g2, fw, ys)


def _pick(n, prefs):
    for p in prefs:
        if n % p == 0:
            return p
    return n


def kernel(x, c, ctx, c_ctx, w_mod, b_mod, norm1_w, norm2_w, final_norm_w, da_w_qkv, da_w_o, da_lambda_q1, da_lambda_k1, da_lambda_q2, da_lambda_k2, da_subln_w, gm_w_in, gm_vnorm_w, gm_vnorm_b, gm_w_s, gm_b_s, gm_w_out, ffn_w_gate, ffn_w_up, ffn_w_down, moe_w_router, moe_w_gate, moe_w_up, moe_w_down):
    b, s_len, d = x.shape
    ctx_len = ctx.shape[1]

    pad = (-(b + 1)) % SUBLANES
    cond = jnp.concatenate([c, c_ctx[None, :], jnp.zeros((pad, d), F32)], axis=0)
    mod = _ada(cond, w_mod, b_mod)

    def chunks(layer, r0, r1):
        m = mod[layer, r0:r1]
        return [m[:, None, i * d:(i + 1) * d] for i in range(6)]

    sh1, sc1, g1, sh2, sc2, g2 = chunks(0, 0, b)
    csh1, csc1 = chunks(0, b, b + 1)[:2]
    sh1b, sc1b, g1b, sh2b, sc2b, g2b = chunks(1, 0, b)

    ts = _pick(ctx_len, ROW_TILES[1:])
    wq = da_w_qkv[0, :, 0:QK_WIDTH]
    wk = da_w_qkv[0, :, QK_WIDTH:2 * QK_WIDTH]
    wv = da_w_qkv[0, :, 2 * QK_WIDTH:]
    k_all, qt, vt5 = _qkv(x, ctx, norm1_w[0][None], sh1, sc1, csh1, csc1,
                          wq.T.astype(BF16), wk.astype(BF16), wv.T.astype(BF16), ts)
    lam_init = 0.8 - 0.6 * math.exp(-0.3 * 0)
    o = _attention(qt, k_all, vt5, da_lambda_q1[0][None], da_lambda_k1[0][None], da_lambda_q2[0][None],
                   da_lambda_k2[0][None], da_subln_w[0][:, None], lam_init, _pick(s_len, (1024,) + ROW_TILES))
    x2 = _ffn(x, o, da_w_o[0].astype(BF16), g1, norm2_w[0][None], sh2, sc2, g2,
              ffn_w_gate[0].astype(BF16), ffn_w_up[0].astype(BF16), ffn_w_down[0].astype(BF16),
              _pick(s_len, ROW_TILES))

    x3, h3, idx, gates = _gmlp(x2, norm1_w[1][None], sh1b, sc1b, g1b, gm_w_in[0].astype(BF16),
                               gm_vnorm_w[0][None], gm_vnorm_b[0][None], gm_w_s[0].astype(BF16),
                               gm_b_s[0][:, :, None], gm_w_out[0].astype(BF16),
                               norm2_w[1][None], sh2b, sc2b, moe_w_router[0].T, _pick(s_len, ROW_TILES))
    n = b * s_len
    ne, _, f = moe_w_gate[0].shape
    tm = _pick(TOP_K * n, (1024,) + ROW_TILES)
    pos, slot_token, tile_expert, tile_used = _route(idx, ne, tm)
    ys = _experts(h3.reshape(n, d), slot_token, tile_expert, tile_used, moe_w_gate[0], moe_w_up[0], moe_w_down[0],
                  tm, _pick(f, ROW_TILES))
    gates_t = jnp.swapaxes(gates, 1, 2).reshape(n, TOP_K)
    out = _combine(x3.reshape(n, d), ys, pos, gates_t, g2b, final_norm_w[None], _pick(s_len, ROW_TILES[1:]))
    return out.reshape(b, s_len, d)
```

```python
import functools
import math

import jax
import jax.numpy as jnp
from jax import lax
from jax.experimental import pallas as pl
from jax.experimental.pallas import tpu as pltpu

F32 = jnp.float32
BF16 = jnp.bfloat16

EPS = 1e-6
GRID_W = 64
ROPE_BASE = 10000.0
N_HEADS = 8
HEAD_DIM = 64
ROPE_FREQS = HEAD_DIM // 4
V_DIM = 2 * HEAD_DIM
QK_WIDTH = N_HEADS * 2 * HEAD_DIM
V_WIDTH = N_HEADS * V_DIM
BF16_SUBLANES = 16
V_ROWS = V_DIM + BF16_SUBLANES
KEY_TILES_PER_BLOCK = (5, 4, 3, 2)
SCORE_BUFFERS = 2
DMA_ISSUE_UNROLL = 4
ROW_TILES = (512, 256, 128)
GM_CHUNK = 128
GM_GROUPS = 8
TOP_K = 2
NEG_BIG = -1e30
LOG2_E = math.log2(math.e)

LANES = 128
SUBLANES = 8
VMEM_LIMIT_BYTES = 56 * 1024 * 1024


def _cparams(semantics):
    return pltpu.CompilerParams(dimension_semantics=semantics, vmem_limit_bytes=VMEM_LIMIT_BYTES)


def _rms(x, w):
    return x * lax.rsqrt(jnp.mean(x * x, axis=-1, keepdims=True) + EPS) * w


def _silu(x):
    return x * jax.nn.sigmoid(x)


def _const_spec(shape):
    nd = len(shape)
    return pl.BlockSpec(shape, lambda *_: (0,) * nd, pipeline_mode=pl.Buffered(1))


def _ada_kernel(cond_ref, w_ref, b_ref, o_ref):
    s = _silu(cond_ref[...])
    o_ref[0] = jnp.dot(s, w_ref[0], preferred_element_type=F32, precision=lax.Precision.HIGHEST) + b_ref[0]


def _ada(cond, w_mod, b_mod):
    depth, d, n = w_mod.shape
    rows = cond.shape[0]
    tn = n // 4
    return pl.pallas_call(
        _ada_kernel,
        out_shape=jax.ShapeDtypeStruct((depth, rows, n), F32),
        grid=(depth, n // tn),
        in_specs=[
            pl.BlockSpec((rows, d), lambda l, j: (0, 0)),
            pl.BlockSpec((1, d, tn), lambda l, j: (l, 0, j)),
            pl.BlockSpec((1, 1, tn), lambda l, j: (l, 0, j)),
        ],
        out_specs=pl.BlockSpec((1, rows, tn), lambda l, j: (l, 0, j)),
        compiler_params=_cparams(("arbitrary", "arbitrary")),
        name="ada",
    )(cond, w_mod, b_mod.reshape(depth, 1, n))


def _qkv_kernel(x_ref, ctx_ref, nw_ref, sh_ref, sc_ref, csh_ref, csc_ref, wqt_ref, wk_ref, wvt_ref,
                ck_ref, sk_ref, cq_ref, sq_ref, k_ref, qt_ref, vt_ref, *, n_ctx, ts):
    is_ctx = pl.program_id(1) < n_ctx
    xin = jnp.where(is_ctx, ctx_ref[0], x_ref[0])
    sh = jnp.where(is_ctx, csh_ref[0], sh_ref[0])
    sc = jnp.where(is_ctx, csc_ref[0], sc_ref[0])
    h = (_rms(xin, nw_ref[...]) * (1.0 + sc) + sh).astype(BF16)

    k = jnp.dot(h, wk_ref[...], preferred_element_type=F32)
    lane = lax.broadcasted_iota(jnp.int32, (ts, LANES), 1)
    lower = (lane % 32) < 16
    ck = ck_ref[...]
    sk = sk_ref[...]
    for j in range(N_HEADS):
        kb = k[:, j * LANES:(j + 1) * LANES]
        partner = jnp.where(lower, pltpu.roll(kb, LANES - 16, 1), pltpu.roll(kb, 16, 1))
        k_ref[0, :, j * LANES:(j + 1) * LANES] = (kb * ck + partner * sk).astype(BF16)

    nt = (((1,), (1,)), ((), ()))
    qt = lax.dot_general(wqt_ref[...], h, nt, preferred_element_type=F32)
    cq = cq_ref[...]
    sq = sq_ref[...]
    scale = HEAD_DIM ** -0.5 * LOG2_E
    for g in range(QK_WIDTH // HEAD_DIM):
        blk = qt[g * HEAD_DIM:(g + 1) * HEAD_DIM]
        partner = jnp.concatenate([blk[16:32], blk[0:16], blk[48:64], blk[32:48]], axis=0)
        qt_ref[0, g * HEAD_DIM:(g + 1) * HEAD_DIM, :] = ((blk * cq + partner * sq) * scale).astype(BF16)

    vt = lax.dot_general(wvt_ref[...], h, nt, preferred_element_type=F32)
    ones_row = (lax.broadcasted_iota(jnp.int32, (V_ROWS - V_DIM, ts), 0) == 0).astype(BF16)
    for hh in range(N_HEADS):
        vt_ref[0, hh, 0, 0:V_DIM, :] = vt[hh * V_DIM:(hh + 1) * V_DIM].astype(BF16)
        vt_ref[0, hh, 0, V_DIM:V_ROWS, :] = ones_row


def _rope_tables(s_len, ctx_len):
    pos = jnp.arange(s_len)
    row = (pos // GRID_W).astype(F32)
    col = (pos % GRID_W).astype(F32)
    inv_freq = ROPE_BASE ** (-jnp.arange(ROPE_FREQS, dtype=F32) / ROPE_FREQS)
    ar = row[:, None] * inv_freq
    ac = col[:, None] * inv_freq
    cos64 = jnp.concatenate([jnp.cos(ar), jnp.cos(ar), jnp.cos(ac), jnp.cos(ac)], axis=-1)
    sin64 = jnp.concatenate([-jnp.sin(ar), jnp.sin(ar), -jnp.sin(ac), jnp.sin(ac)], axis=-1)
    ck = jnp.concatenate([jnp.ones((ctx_len, LANES), F32), jnp.tile(cos64, (1, 2))], axis=0)
    sk = jnp.concatenate([jnp.zeros((ctx_len, LANES), F32), jnp.tile(sin64, (1, 2))], axis=0)
    return ck, sk, cos64.T, sin64.T


def _qkv(x, ctx, nw, sh, sc, csh, csc, wqt, wk, wvt, ts):
    b, s_len, d = x.shape
    ctx_len = ctx.shape[1]
    assert s_len % ts == 0 and ctx_len % ts == 0
    n_ctx = ctx_len // ts
    n_lat = s_len // ts
    ktot = ctx_len + s_len
    sub = _pick(n_ctx + n_lat, KEY_TILES_PER_BLOCK)
    ck, sk, cq, sq = _rope_tables(s_len, ctx_len)

    def lat(t):
        return jnp.maximum(t - n_ctx, 0)

    row = lambda bb, t: (bb, 0, 0)
    return pl.pallas_call(
        functools.partial(_qkv_kernel, n_ctx=n_ctx, ts=ts),
        out_shape=(
            jax.ShapeDtypeStruct((b, ktot, QK_WIDTH), BF16),
            jax.ShapeDtypeStruct((b, QK_WIDTH, s_len), BF16),
            jax.ShapeDtypeStruct((b, N_HEADS, ktot // (ts * sub), V_ROWS, ts * sub), BF16),
        ),
        grid=(b, n_ctx + n_lat),
        in_specs=[
            pl.BlockSpec((1, ts, d), lambda bb, t: (bb, lat(t), 0)),
            pl.BlockSpec((1, ts, d), lambda bb, t: (bb, jnp.minimum(t, n_ctx - 1), 0)),
            _const_spec((1, d)),
            pl.BlockSpec((1, 1, d), row),
            pl.BlockSpec((1, 1, d), row),
            _const_spec((1, 1, d)),
            _const_spec((1, 1, d)),
            _const_spec((QK_WIDTH, d)),
            _const_spec((d, QK_WIDTH)),
            _const_spec((V_WIDTH, d)),
            pl.BlockSpec((ts, LANES), lambda bb, t: (t, 0)),
            pl.BlockSpec((ts, LANES), lambda bb, t: (t, 0)),
            pl.BlockSpec((HEAD_DIM, ts), lambda bb, t: (0, lat(t))),
            pl.BlockSpec((HEAD_DIM, ts), lambda bb, t: (0, lat(t))),
        ],
        out_specs=(
            pl.BlockSpec((1, ts, QK_WIDTH), lambda bb, t: (bb, t, 0)),
            pl.BlockSpec((1, QK_WIDTH, ts), lambda bb, t: (bb, 0, lat(t))),
            pl.BlockSpec((1, N_HEADS, 1, V_ROWS, ts), lambda bb, t: (bb, 0, t // sub, 0, t % sub)),
        ),
        compiler_params=_cparams(("arbitrary", "arbitrary")),
        name="qkv",
    )(x, ctx, nw, sh, sc, csh, csc, wqt, wk, wvt, ck, sk, cq, sq)


def _attn_kernel(lq1_ref, lk1_ref, lq2_ref, lk2_ref, subw_ref, q_ref, k_ref, v_ref, o_ref, w_scr, acc_scr,
                 *bufs, tq, tkb, nb, lam_init):
    nbuf = len(bufs) // 2
    s_bufs, p_bufs = bufs[:nbuf], bufs[nbuf:]
    q = q_ref[0]
    zero = jnp.zeros((HEAD_DIM, tq), BF16)
    w_scr[0:HEAD_DIM, 0:tq] = q[0:HEAD_DIM]
    w_scr[0:HEAD_DIM, tq:2 * tq] = zero
    w_scr[HEAD_DIM:2 * HEAD_DIM, 0:tq] = zero
    w_scr[HEAD_DIM:2 * HEAD_DIM, tq:2 * tq] = q[HEAD_DIM:2 * HEAD_DIM]
    acc_scr[...] = jnp.zeros_like(acc_scr)

    def score_block(blk, slot):
        off = pl.multiple_of(blk * tkb, tkb)
        s = jnp.dot(k_ref[0, pl.ds(off, tkb), :], w_scr[...], preferred_element_type=F32)
        s_bufs[slot][...] = s
        return jnp.max(s, axis=0, keepdims=True)

    def softmax(slot, m, block_max):
        m_new = jnp.maximum(m, block_max)
        p_bufs[slot][...] = jnp.exp2(s_bufs[slot][...] - m_new).astype(BF16)
        return m_new, jnp.exp2(m - m_new)

    def values(j, slot, alpha):
        pv = jnp.dot(v_ref[0, 0, j], p_bufs[slot][...], preferred_element_type=F32)
        acc_scr[...] = acc_scr[...] * alpha + pv

    def step(j, slot, m, alpha, block_max):
        next_max = score_block(j + 2, (slot + 2) % nbuf)
        m, alpha_next = softmax((slot + 1) % nbuf, m, block_max)
        values(j, slot, alpha)
        return m, alpha_next, next_max

    m, alpha = softmax(0, jnp.full((1, 2 * tq), NEG_BIG, F32), score_block(0, 0))
    block_max = score_block(1, 1) if nb > 1 else None
    n_full = max(nb - 2, 0)

    def unrolled(i, carry):
        for r in range(nbuf):
            carry = step(nbuf * i + r, r, *carry)
        return carry

    if n_full:
        m, alpha, block_max = lax.fori_loop(0, n_full // nbuf, unrolled, (m, alpha, block_max))
    for j in range(nbuf * (n_full // nbuf), n_full):
        m, alpha, block_max = step(j, j % nbuf, m, alpha, block_max)
    if nb > 1:
        m, alpha_next = softmax((nb - 1) % nbuf, m, block_max)
        values(nb - 2, (nb - 2) % nbuf, alpha)
        alpha = alpha_next
    values(nb - 1, (nb - 1) % nbuf, alpha)

    lam = (jnp.exp(jnp.sum(lq1_ref[...] * lk1_ref[...], axis=-1, keepdims=True))
           - jnp.exp(jnp.sum(lq2_ref[...] * lk2_ref[...], axis=-1, keepdims=True)) + lam_init)
    acc = acc_scr[0:V_DIM, :]
    inv = 1.0 / acc_scr[V_DIM:V_DIM + 1, :]
    o = acc[:, 0:tq] * inv[:, 0:tq] - lam * (acc[:, tq:2 * tq] * inv[:, tq:2 * tq])
    y = o * lax.rsqrt(jnp.mean(o * o, axis=0, keepdims=True) + EPS) * subw_ref[...]
    y = y * (1.0 - lam_init)
    o_ref[0] = y.T.astype(BF16)


def _attention(qt, k_all, vt5, lq1, lk1, lq2, lk2, subw, lam_init, tq):
    b, _, s_len = qt.shape
    ktot = k_all.shape[1]
    nb, tkb = vt5.shape[2], vt5.shape[4]
    assert s_len % tq == 0 and nb * tkb == ktot and vt5.shape[3] == V_ROWS
    vec = _const_spec((1, HEAD_DIM))
    return pl.pallas_call(
        functools.partial(_attn_kernel, tq=tq, tkb=tkb, nb=nb, lam_init=lam_init),
        out_shape=jax.ShapeDtypeStruct((b, s_len, V_WIDTH), BF16),
        grid=(b, N_HEADS, s_len // tq),
        in_specs=[
            vec, vec, vec, vec,
            _const_spec((V_DIM, 1)),
            pl.BlockSpec((1, 2 * HEAD_DIM, tq), lambda bb, hh, i: (bb, hh, i)),
            pl.BlockSpec((1, ktot, 2 * HEAD_DIM), lambda bb, hh, i: (bb, 0, hh), pipeline_mode=pl.Buffered(1)),
            pl.BlockSpec((1, 1, nb, V_ROWS, tkb), lambda bb, hh, i: (bb, hh, 0, 0, 0),
                         pipeline_mode=pl.Buffered(1)),
        ],
        out_specs=pl.BlockSpec((1, tq, V_DIM), lambda bb, hh, i: (bb, i, hh)),
        scratch_shapes=[
            pltpu.VMEM((2 * HEAD_DIM, 2 * tq), BF16),
            pltpu.VMEM((V_ROWS, 2 * tq), F32),
        ] + [pltpu.VMEM((tkb, 2 * tq), F32)] * SCORE_BUFFERS + [pltpu.VMEM((tkb, 2 * tq), BF16)] * SCORE_BUFFERS,
        compiler_params=_cparams(("arbitrary", "arbitrary", "arbitrary")),
        name="attention",
    )(lq1, lk1, lq2, lk2, subw, qt, k_all, vt5)


def _ffn_kernel(x_ref, o_ref, wo_ref, g1_ref, nw_ref, sh_ref, sc_ref, g2_ref, wg_ref, wu_ref, wd_ref, out_ref):
    y = jnp.dot(o_ref[0], wo_ref[...], preferred_element_type=F32)
    x1 = x_ref[0] + g1_ref[0] * y
    h = (_rms(x1, nw_ref[...]) * (1.0 + sc_ref[0]) + sh_ref[0]).astype(BF16)
    a = jnp.dot(h, wg_ref[...], preferred_element_type=F32)
    u = jnp.dot(h, wu_ref[...], preferred_element_type=F32)
    z = (_silu(a) * u).astype(BF16)
    out_ref[0] = x1 + g2_ref[0] * jnp.dot(z, wd_ref[...], preferred_element_type=F32)


def _ffn(x, o, wo, g1, nw, sh, sc, g2, wg, wu, wd, ts):
    b, s_len, d = x.shape
    f = wg.shape[1]
    row = lambda bb, t: (bb, 0, 0)
    tile = lambda bb, t: (bb, t, 0)
    vec = pl.BlockSpec((1, 1, d), row)
    return pl.pallas_call(
        _ffn_kernel,
        out_shape=jax.ShapeDtypeStruct((b, s_len, d), F32),
        grid=(b, s_len // ts),
        in_specs=[
            pl.BlockSpec((1, ts, d), tile),
            pl.BlockSpec((1, ts, V_WIDTH), tile),
            _const_spec((V_WIDTH, d)),
            vec, _const_spec((1, d)), vec, vec, vec,
            _const_spec((d, f)), _const_spec((d, f)), _const_spec((f, d)),
        ],
        out_specs=pl.BlockSpec((1, ts, d), tile),
        compiler_params=_cparams(("arbitrary", "arbitrary")),
        name="ffn",
    )(x, o, wo, g1, nw, sh, sc, g2, wg, wu, wd)


def _gmlp_kernel(x_ref, nw1_ref, sh1_ref, sc1_ref, g1_ref, win_ref, vnw_ref, vnb_ref, ws_ref, bs_ref, wout_ref,
                 nw2_ref, sh2_ref, sc2_ref, wrt_ref, x3_ref, h3_ref, idx_ref, gate_ref, t_scr, *, ts, half):
    x2 = x_ref[0]
    h = (_rms(x2, nw1_ref[...]) * (1.0 + sc1_ref[0]) + sh1_ref[0]).astype(BF16)
    z = jnp.dot(h, win_ref[...], preferred_element_type=F32)
    z = 0.5 * z * (1.0 + lax.erf(z * (2.0 ** -0.5)))
    u = z[:, 0:half]
    v = z[:, half:2 * half]
    mu = jnp.mean(v, axis=-1, keepdims=True)
    vc = v - mu
    v = vc * lax.rsqrt(jnp.mean(vc * vc, axis=-1, keepdims=True) + EPS) * vnw_ref[...] + vnb_ref[...]
    vb = v.astype(BF16)
    gd = half // GM_GROUPS
    for n in range(ts // GM_CHUNK):
        rows = slice(n * GM_CHUNK, (n + 1) * GM_CHUNK)
        for g in range(GM_GROUPS):
            cols = slice(g * gd, (g + 1) * gd)
            s = jnp.dot(ws_ref[g], vb[rows, cols], preferred_element_type=F32) + bs_ref[g]
            t_scr[rows, cols] = (u[rows, cols] * s).astype(BF16)
    x3 = x2 + g1_ref[0] * jnp.dot(t_scr[...], wout_ref[...], preferred_element_type=F32)
    x3_ref[0] = x3

    h3 = _rms(x3, nw2_ref[...]) * (1.0 + sc2_ref[0]) + sh2_ref[0]
    h3_ref[0] = h3
    nt = (((1,), (1,)), ((), ()))
    logits = lax.dot_general(wrt_ref[...], h3, nt, preferred_element_type=F32,
                             precision=lax.Precision.HIGHEST)
    ne = logits.shape[0]
    eid = lax.broadcasted_iota(jnp.int32, logits.shape, 0)
    m1 = jnp.max(logits, axis=0, keepdims=True)
    i1 = jnp.min(jnp.where(logits == m1, eid, ne), axis=0, keepdims=True)
    rest = jnp.where(eid == i1, -jnp.inf, logits)
    m2 = jnp.max(rest, axis=0, keepdims=True)
    i2 = jnp.min(jnp.where(rest == m2, eid, ne), axis=0, keepdims=True)
    e2 = jnp.exp(m2 - m1)
    den = 1.0 + e2
    idx_ref[0] = jnp.concatenate([i1, i2], axis=0)
    gate_ref[0] = jnp.concatenate([1.0 / den, e2 / den], axis=0)


def _gmlp(x, nw1, sh1, sc1, g1, win, vnw, vnb, ws, bs, wout, nw2, sh2, sc2, wrt, ts):
    b, s_len, d = x.shape
    half = wout.shape[0]
    ne = wrt.shape[0]
    assert ts % GM_CHUNK == 0 and s_len % ts == 0
    row = lambda bb, t: (bb, 0, 0)
    tile = lambda bb, t: (bb, t, 0)
    vec = pl.BlockSpec((1, 1, d), row)
    return pl.pallas_call(
        functools.partial(_gmlp_kernel, ts=ts, half=half),
        out_shape=(
            jax.ShapeDtypeStruct((b, s_len, d), F32),
            jax.ShapeDtypeStruct((b, s_len, d), F32),
            jax.ShapeDtypeStruct((b, TOP_K, s_len), jnp.int32),
            jax.ShapeDtypeStruct((b, TOP_K, s_len), F32),
        ),
        grid=(b, s_len // ts),
        in_specs=[
            pl.BlockSpec((1, ts, d), tile),
            _const_spec((1, d)), vec, vec, vec,
            _const_spec((d, 2 * half)),
            _const_spec((1, half)), _const_spec((1, half)),
            _const_spec((GM_GROUPS, GM_CHUNK, GM_CHUNK)),
            _const_spec((GM_GROUPS, GM_CHUNK, 1)),
            _const_spec((half, d)),
            _const_spec((1, d)), vec, vec,
            _const_spec((ne, d)),
        ],
        out_specs=(
            pl.BlockSpec((1, ts, d), tile),
            pl.BlockSpec((1, ts, d), tile),
            pl.BlockSpec((1, TOP_K, ts), lambda bb, t: (bb, 0, t)),
            pl.BlockSpec((1, TOP_K, ts), lambda bb, t: (bb, 0, t)),
        ),
        scratch_shapes=[pltpu.VMEM((ts, half), BF16)],
        compiler_params=_cparams(("arbitrary", "arbitrary")),
        name="gmlp",
    )(x, nw1, sh1, sc1, g1, win, vnw, vnb, ws, bs, wout, nw2, sh2, sc2, wrt)


def _route(idx, n_experts, tm):
    b, k, s_len = idx.shape
    n = b * s_len
    e_flat = jnp.swapaxes(idx, 0, 1).reshape(k * n)
    onehot = (e_flat[:, None] == jnp.arange(n_experts, dtype=jnp.int32)[None, :]).astype(jnp.int32)
    csum = jnp.cumsum(onehot, axis=0)
    counts = csum[-1]
    rank = jnp.take_along_axis(csum, e_flat[:, None], axis=1)[:, 0] - 1
    padded = ((counts + tm - 1) // tm) * tm
    ends = jnp.cumsum(padded)
    starts = ends - padded
    pos = (starts[e_flat] + rank).astype(jnp.int32)
    n_tiles = (k * n) // tm + n_experts + 1
    tile_ids = jnp.arange(n_tiles, dtype=jnp.int32)
    used = tile_ids * tm < ends[-1]
    tile_src = jnp.where(used, tile_ids, ends[-1] // tm - 1).astype(jnp.int32)
    tile_expert = jnp.minimum(jnp.searchsorted(ends, tile_src * tm, side="right"), n_experts - 1).astype(jnp.int32)
    order = jnp.argsort(e_flat, stable=True).astype(jnp.int32)
    slot_expert = jnp.repeat(tile_expert, tm)
    slot_rank = jnp.arange(n_tiles * tm, dtype=jnp.int32) - starts[slot_expert]
    valid = jnp.repeat(used, tm) & (slot_rank < counts[slot_expert])
    src = jnp.clip((jnp.cumsum(counts) - counts)[slot_expert] + slot_rank, 0, k * n - 1)
    slot_token = jnp.where(valid, order[src] % n, 0).astype(jnp.int32)
    return pos.reshape(k, n), slot_token, tile_expert, used.astype(jnp.int32)


def _experts_kernel(exp_ref, used_ref, tok_ref, tokn_ref, h_ref, wg_ref, wu_ref, wd_ref, y_ref,
                    xbuf, xb_scr, acc_scr, sems, *, tm, per_step):
    j = pl.program_id(0)
    f = pl.program_id(1)
    n_rows = per_step * pl.num_programs(1)
    used = used_ref[j] > 0
    slot = j % 2

    def row_copy(t_ref, r, sl):
        return pltpu.make_async_copy(h_ref.at[pl.ds(t_ref[0, 0, r], 1)], xbuf.at[sl, pl.ds(r, 1)], sems.at[sl])

    @pl.when(f == 0)
    def _():
        @pl.when(j == 0)
        def _():
            def body(r, carry):
                row_copy(tok_ref, r, 0).start()
                return carry
            lax.fori_loop(0, n_rows, body, 0, unroll=DMA_ISSUE_UNROLL)

        @pl.when(used_ref[jnp.maximum(j - 1, 0)] > 0)
        def _():
            pltpu.make_async_copy(h_ref.at[pl.ds(0, n_rows)], xbuf.at[slot, pl.ds(0, n_rows)], sems.at[slot]).wait()
            xb_scr[...] = xbuf[slot, 0:tm].astype(BF16)

        acc_scr[...] = jnp.zeros_like(acc_scr)

    @pl.when(used)
    def _():
        for i in range(per_step):
            row_copy(tokn_ref, f * per_step + i, 1 - slot).start()
        xb = xb_scr[...]
        a = jnp.dot(xb, wg_ref[0].astype(BF16), preferred_element_type=F32)
        u = jnp.dot(xb, wu_ref[0].astype(BF16), preferred_element_type=F32)
        z = (_silu(a) * u).astype(BF16)
        acc_scr[...] += jnp.dot(z, wd_ref[0].astype(BF16), preferred_element_type=F32)

    @pl.when(f == pl.num_programs(1) - 1)
    def _():
        y_ref[...] = acc_scr[...]


def _experts(h, slot_token, tile_expert, tile_used, wg, wu, wd, tm, tf):
    n, d = h.shape
    ne, _, f = wg.shape
    nf = f // tf
    n_tiles = tile_expert.shape[0]
    per_step = -(-tm // (nf * SUBLANES)) * SUBLANES
    n_rows = per_step * nf
    assert slot_token.shape[0] == n_tiles * tm and f % tf == 0 and n_rows <= n
    tok3 = jnp.pad(slot_token.reshape(n_tiles, 1, tm), ((0, 0), (0, 0), (0, n_rows - tm)))

    def fchunk(j, fi, used):
        return jnp.where(used[j] > 0, fi, nf - 1)

    smem_tile = functools.partial(pl.BlockSpec, (1, 1, n_rows), memory_space=pltpu.SMEM)
    return pl.pallas_call(
        functools.partial(_experts_kernel, tm=tm, per_step=per_step),
        out_shape=jax.ShapeDtypeStruct((n_tiles * tm, d), F32),
        grid_spec=pltpu.PrefetchScalarGridSpec(
            num_scalar_prefetch=2,
            grid=(n_tiles, nf),
            in_specs=[
                smem_tile(lambda j, fi, exp, used: (j, 0, 0)),
                smem_tile(lambda j, fi, exp, used: (jnp.minimum(j + 1, n_tiles - 1), 0, 0)),
                pl.BlockSpec(memory_space=pl.ANY),
                pl.BlockSpec((1, d, tf), lambda j, fi, exp, used: (exp[j], 0, fchunk(j, fi, used))),
                pl.BlockSpec((1, d, tf), lambda j, fi, exp, used: (exp[j], 0, fchunk(j, fi, used))),
                pl.BlockSpec((1, tf, d), lambda j, fi, exp, used: (exp[j], fchunk(j, fi, used), 0)),
            ],
            out_specs=pl.BlockSpec((tm, d), lambda j, fi, exp, used: (j, 0)),
            scratch_shapes=[pltpu.VMEM((2, n_rows, d), F32), pltpu.VMEM((tm, d), BF16),
                            pltpu.VMEM((tm, d), F32), pltpu.SemaphoreType.DMA((2,))],
        ),
        compiler_params=_cparams(("arbitrary", "arbitrary")),
        name="moe_experts",
    )(tile_expert, tile_used, tok3, tok3, h, wg, wu, wd)


def _combine_kernel(pos_ref, posn_ref, x_ref, gate_ref, g2_ref, fw_ref, ys_ref, out_ref, ybuf, sems, *, rows):
    i = pl.program_id(0)
    slot = i % 2

    def issue(p_ref, sl):
        def body(r, carry):
            for k in range(TOP_K):
                pltpu.make_async_copy(ys_ref.at[pl.ds(p_ref[0, k, r], 1)], ybuf.at[sl, pl.ds(k * rows + r, 1)],
                                      sems.at[sl]).start()
            return carry
        lax.fori_loop(0, rows, body, 0, unroll=DMA_ISSUE_UNROLL)

    @pl.when(i == 0)
    def _():
        issue(pos_ref, 0)

    @pl.when(i + 1 < pl.num_programs(0))
    def _():
        issue(posn_ref, 1 - slot)

    pltpu.make_async_copy(ys_ref.at[pl.ds(0, TOP_K * rows)], ybuf.at[slot], sems.at[slot]).wait()
    g = gate_ref[...]
    y = g[:, 0:1] * ybuf[slot, 0:rows] + g[:, 1:2] * ybuf[slot, rows:2 * rows]
    out_ref[...] = _rms(x_ref[...] + g2_ref[0] * y, fw_ref[...])


def _combine(x, ys, pos, gates, g2, fw, rows):
    n, d = x.shape
    n_steps = n // rows
    steps_per_batch = n_steps // g2.shape[0]
    assert n % rows == 0 and n_steps % g2.shape[0] == 0
    pos3 = jnp.swapaxes(pos.reshape(TOP_K, n_steps, rows), 0, 1)
    smem_tile = functools.partial(pl.BlockSpec, (1, TOP_K, rows), memory_space=pltpu.SMEM)
    return pl.pallas_call(
        functools.partial(_combine_kernel, rows=rows),
        out_shape=jax.ShapeDtypeStruct((n, d), F32),
        grid=(n_steps,),
        in_specs=[
            smem_tile(lambda i: (i, 0, 0)),
            smem_tile(lambda i: (jnp.minimum(i + 1, n_steps - 1), 0, 0)),
            pl.BlockSpec((rows, d), lambda i: (i, 0)),
            pl.BlockSpec((rows, TOP_K), lambda i: (i, 0)),
            pl.BlockSpec((1, 1, d), lambda i: (i // steps_per_batch, 0, 0)),
            pl.BlockSpec((1, d), lambda i: (0, 0)),
            pl.BlockSpec(memory_space=pl.ANY),
        ],
        out_specs=pl.BlockSpec((rows, d), lambda i: (i, 0)),
        scratch_shapes=[pltpu.VMEM((2, TOP_K * rows, d), F32), pltpu.SemaphoreType.DMA((2,))],
        compiler_params=_cparams(("arbitrary",)),
        name="moe_combine",
    )(pos3, pos3, x, gates, g2, fw, ys)


def _pick(n, prefs):
    for p in prefs:
        if n % p == 0:
            return p
    return n


def kernel(x, c, ctx, c_ctx, w_mod, b_mod, norm1_w, norm2_w, final_norm_w, da_w_qkv, da_w_o, da_lambda_q1, da_lambda_k1, da_lambda_q2, da_lambda_k2, da_subln_w, gm_w_in, gm_vnorm_w, gm_vnorm_b, gm_w_s, gm_b_s, gm_w_out, ffn_w_gate, ffn_w_up, ffn_w_down, moe_w_router, moe_w_gate, moe_w_up, moe_w_down):
    b, s_len, d = x.shape
    ctx_len = ctx.shape[1]

    pad = (-(b + 1)) % SUBLANES
    cond = jnp.concatenate([c, c_ctx[None, :], jnp.zeros((pad, d), F32)], axis=0)
    mod = _ada(cond, w_mod, b_mod)

    def chunks(layer, r0, r1):
        m = mod[layer, r0:r1]
        return [m[:, None, i * d:(i + 1) * d] for i in range(6)]

    sh1, sc1, g1, sh2, sc2, g2 = chunks(0, 0, b)
    csh1, csc1 = chunks(0, b, b + 1)[:2]
    sh1b, sc1b, g1b, sh2b, sc2b, g2b = chunks(1, 0, b)

    ts = _pick(ctx_len, ROW_TILES[1:])
    wq = da_w_qkv[0, :, 0:QK_WIDTH]
    wk = da_w_qkv[0, :, QK_WIDTH:2 * QK_WIDTH]
    wv = da_w_qkv[0, :, 2 * QK_WIDTH:]
    k_all, qt, vt5 = _qkv(x, ctx, norm1_w[0][None], sh1, sc1, csh1, csc1,
                          wq.T.astype(BF16), wk.astype(BF16), wv.T.astype(BF16), ts)
    lam_init = 0.8 - 0.6 * math.exp(-0.3 * 0)
    o = _attention(qt, k_all, vt5, da_lambda_q1[0][None], da_lambda_k1[0][None], da_lambda_q2[0][None],
                   da_lambda_k2[0][None], da_subln_w[0][:, None], lam_init, _pick(s_len, (1024,) + ROW_TILES))
    x2 = _ffn(x, o, da_w_o[0].astype(BF16), g1, norm2_w[0][None], sh2, sc2, g2,
              ffn_w_gate[0].astype(BF16), ffn_w_up[0].astype(BF16), ffn_w_down[0].astype(BF16),
              _pick(s_len, ROW_TILES))

    x3, h3, idx, gates = _gmlp(x2, norm1_w[1][None], sh1b, sc1b, g1b, gm_w_in[0].astype(BF16),
                               gm_vnorm_w[0][None], gm_vnorm_b[0][None], gm_w_s[0].astype(BF16),
                               gm_b_s[0][:, :, None], gm_w_out[0].astype(BF16),
                               norm2_w[1][None], sh2b, sc2b, moe_w_router[0].T, _pick(s_len, ROW_TILES))
    n = b * s_len
    ne, _, f = moe_w_gate[0].shape
    tm = _pick(TOP_K * n, (1024,) + ROW_TILES)
    pos, slot_token, tile_expert, tile_used = _route(idx, ne, tm)
    ys = _experts(h3.reshape(n, d), slot_token, tile_expert, tile_used, moe_w_gate[0], moe_w_up[0], moe_w_down[0],
                  tm, _pick(f, ROW_TILES))
    gates_t = jnp.swapaxes(gates, 1, 2).reshape(n, TOP_K)
    out = _combine(x3.reshape(n, d), ys, pos, gates_t, g2b, final_norm_w[None], _pick(s_len, ROW_TILES))
    return out.reshape(b, s_len, d)
```
